```python
import math
import jax
import jax.numpy as jnp
from jax import lax
import numpy as np

D_MODEL = 1024
BATCH = 8
SEQ = 2048
DEPTH = 4
DEC_BATCH = 32
DEC_SEQ = 4
PAST_LEN = 16384
PAGE_SIZE = 128

HEAD_DIM = 64
SSD_WIDTH = D_MODEL // 2
SSD_HEADS = SSD_WIDTH // HEAD_DIM
SSD_HEAD_DIM = HEAD_DIM
SSD_GROUPS = 2
SSD_STATE = 128
SSD_CONV = 4
SSD_CONV_CH = SSD_WIDTH + 2 * SSD_GROUPS * SSD_STATE
SSD_CHUNK = 128
GLA_WIDTH = D_MODEL // 4
GLA_HEADS = GLA_WIDTH // HEAD_DIM
GLA_DV = HEAD_DIM
GLA_DK = HEAD_DIM // 2
GLA_GATE_RANK = 16
GLA_TAU = 16.0
GLA_CHUNK = 64
MLA_WIDTH = D_MODEL - SSD_WIDTH - GLA_WIDTH
MLA_HEADS = MLA_WIDTH // HEAD_DIM
MLA_NOPE = HEAD_DIM
MLA_ROPE = HEAD_DIM // 2
MLA_DV = HEAD_DIM
MLA_Q_LORA = 192
MLA_KV_LORA = 128
MLA_SCALE = (MLA_NOPE + MLA_ROPE) ** -0.5
ROPE_THETA = 10000.0
Q_BLOCK = 128
D_FF = 2816
FFN_CONV = 3
PLE_DIM = 256
DN_ALPHA = (2 * DEPTH) ** 0.25
DN_BETA = (8 * DEPTH) ** -0.25
IN_SPLITS = (SSD_WIDTH, SSD_CONV_CH, SSD_HEADS,
             GLA_HEADS * GLA_DK, GLA_HEADS * GLA_DK, GLA_WIDTH, GLA_WIDTH, GLA_GATE_RANK,
             MLA_Q_LORA, MLA_KV_LORA, MLA_ROPE)
IN_COLS = sum(IN_SPLITS)

kernel_name = "hymba_ssd_gla_mla_deepnorm_step"


def _chunks(a, c):
    b, l = a.shape[:2]
    return a.reshape(b, l // c, c, *a.shape[2:]).swapaxes(0, 1)


def _unchunks(a):
    n, b, c = a.shape[:3]
    return a.swapaxes(0, 1).reshape(b, n * c, *a.shape[3:])


def _layernorm(x, g, b, eps=1e-5):
    xf = x.astype(jnp.float32)
    xc = xf - jnp.mean(xf, -1, keepdims=True)
    var = jnp.mean(xc * xc, -1, keepdims=True)
    return (xc * lax.rsqrt(var + eps)).astype(x.dtype) * g + b


def _rmsnorm(x, g, eps=1e-6):
    xf = x.astype(jnp.float32)
    return (xf * lax.rsqrt(jnp.mean(xf * xf, -1, keepdims=True) + eps)).astype(x.dtype) * g


def _rope(x, pos):
    half = x.shape[-1] // 2
    inv = ROPE_THETA ** (-jnp.arange(half, dtype=jnp.float32) / half)
    ang = pos.astype(jnp.float32)[:, None] * inv
    ang = ang.reshape(ang.shape[0], *([1] * (x.ndim - 3)), half)
    cos = jnp.cos(ang).astype(x.dtype)
    sin = jnp.sin(ang).astype(x.dtype)
    x1, x2 = x[..., :half], x[..., half:]
    return jnp.concatenate([x1 * cos - x2 * sin, x1 * sin + x2 * cos], axis=-1)


def _causal_dwconv(full, w, b):
    c = full.shape[-1]
    out = lax.conv_general_dilated(full, w[:, None, :], window_strides=(1,), padding="VALID",
                                   dimension_numbers=("NWC", "WIO", "NWC"), feature_group_count=c)
    return out + b


def _ssd_scan(x, dt, a_neg, bm, cm, h0):
    l = x.shape[1]
    c = math.gcd(l, SSD_CHUNK)
    rep = x.shape[2] // bm.shape[2]
    bh = jnp.repeat(bm, rep, axis=2)
    ch = jnp.repeat(cm, rep, axis=2)
    la = dt * a_neg
    causal = jnp.tril(jnp.ones((c, c), dtype=bool))

    def step(h, inp):
        xc, dtc, lac, bc, cc = inp
        cum = jnp.cumsum(lac, axis=1)
        seg = cum[:, :, None, :] - cum[:, None, :, :]
        decay = jnp.exp(jnp.where(causal[None, :, :, None], seg, -jnp.inf))
        scores = jnp.einsum("bihn,bjhn->bijh", cc, bc) * decay
        y = jnp.einsum("bijh,bjh,bjhp->bihp", scores, dtc, xc)
        y = y + jnp.einsum("bihn,bhpn->bihp", cc, h) * jnp.exp(cum)[..., None]
        w = jnp.exp(cum[:, -1:, :] - cum) * dtc
        h = h * jnp.exp(cum[:, -1, :])[:, :, None, None] + jnp.einsum("bjh,bjhn,bjhp->bhpn", w, bc, xc)
        return h, y

    h, ys = lax.scan(step, h0, (_chunks(x, c), _chunks(dt, c), _chunks(la, c), _chunks(bh, c), _chunks(ch, c)))
    return _unchunks(ys), h


def _gla_scan(q, k, v, log_a, s0):
    l = q.shape[1]
    c = math.gcd(l, GLA_CHUNK)
    causal = jnp.tril(jnp.ones((c, c), dtype=bool))

    def step(s, inp):
        qc, kc, vc, gc = inp
        b = jnp.cumsum(gc, axis=1)
        o = jnp.einsum("bihk,bhkv->bihv", qc * jnp.exp(b), s)
        diff = b[:, :, None] - b[:, None, :]
        decay = jnp.exp(jnp.where(causal[None, :, :, None, None], diff, -jnp.inf))
        att = jnp.einsum("bihk,bjhk,bijhk->bijh", qc, kc, decay)
        o = o + jnp.einsum("bijh,bjhv->bihv", att, vc)
        b_last = b[:, -1]
        s = s * jnp.exp(b_last)[..., None] + jnp.einsum("bjhk,bjhv->bhkv", kc * jnp.exp(b_last[:, None] - b), vc)
        return s, o

    s, os_ = lax.scan(step, s0, (_chunks(q, c), _chunks(k, c), _chunks(v, c), _chunks(log_a, c)))
    return _unchunks(os_), s


def _mla_attend(q_lat, q_rope, kv, kr, t_past):
    l = q_lat.shape[1]
    qb = math.gcd(l, Q_BLOCK)
    k_pos = jnp.arange(kv.shape[1])
    q_pos = t_past + jnp.arange(l)

    def block(inp):
        ql, qr, qp = inp
        s = jnp.einsum("bqhr,bkr->bhqk", ql, kv) + jnp.einsum("bqhd,bkd->bhqk", qr, kr)
        s = s.astype(jnp.float32) * MLA_SCALE
        s = jnp.where(k_pos[None, None, None, :] <= qp[None, None, :, None], s, -jnp.inf)
        p = jax.nn.softmax(s, axis=-1).astype(kv.dtype)
        return jnp.einsum("bhqk,bkr->bqhr", p, kv)

    out = lax.map(block, (_chunks(q_lat, qb), _chunks(q_rope, qb), q_pos.reshape(-1, qb)))
    return _unchunks(out)


def _layer(x, pl, ssm_h0, ssm_buf0, gla_s0, ffn_buf0, kv_past, kr_past, lw):
    bsz, l, _ = x.shape
    t_past = kv_past.shape[1]
    pos = t_past + jnp.arange(l)
    proj = x @ lw["w_in"]
    (z, xbc, dt_raw, q_raw, k_raw, v_raw, r_raw, g_lr, cq, ckv, kr) = jnp.split(
        proj, np.cumsum(IN_SPLITS)[:-1].tolist(), axis=-1)

    full = jnp.concatenate([ssm_buf0, xbc], axis=1)
    ssm_buf = full[:, full.shape[1] - (SSD_CONV - 1):]
    xbc = jax.nn.silu(_causal_dwconv(full, lw["ssd_conv_w"], lw["ssd_conv_b"]))
    xs, bs, cs = jnp.split(xbc, [SSD_WIDTH, SSD_WIDTH + SSD_GROUPS * SSD_STATE], axis=-1)
    xs = xs.reshape(bsz, l, SSD_HEADS, SSD_HEAD_DIM)
    bs = bs.reshape(bsz, l, SSD_GROUPS, SSD_STATE)
    cs = cs.reshape(bsz, l, SSD_GROUPS, SSD_STATE)
    dt = jax.nn.softplus(dt_raw + lw["ssd_dt_bias"])
    a_neg = -jnp.exp(lw["ssd_a_log"])
    y, ssm_h = _ssd_scan(xs, dt, a_neg, bs, cs, ssm_h0)
    y = (y + lw["ssd_d"][:, None] * xs).reshape(bsz, l, SSD_WIDTH) * jax.nn.silu(z)
    y_ssd = _rmsnorm(y.reshape(bsz, l, SSD_GROUPS, -1),
                     lw["ssd_norm_g"].reshape(SSD_GROUPS, -1)).reshape(bsz, l, SSD_WIDTH)

    q = q_raw.reshape(bsz, l, GLA_HEADS, GLA_DK) * GLA_DK ** -0.5
    k = k_raw.reshape(bsz, l, GLA_HEADS, GLA_DK)
    v = v_raw.reshape(bsz, l, GLA_HEADS, GLA_DV)
    log_a = (jax.nn.log_sigmoid(g_lr @ lw["gla_gate_w"] + lw["gla_gate_b"]) / GLA_TAU).reshape(bsz, l, GLA_HEADS, GLA_DK)
    o, gla_s = _gla_scan(q, k, v, log_a, gla_s0)
    y_gla = (_rmsnorm(o, lw["gla_norm_g"]) * jax.nn.silu(r_raw.reshape(bsz, l, GLA_HEADS, GLA_DV))).reshape(bsz, l, GLA_WIDTH)

    qf = (_rmsnorm(cq, lw["mla_q_norm_g"]) @ lw["mla_w_uq"]).reshape(bsz, l, MLA_HEADS, MLA_NOPE + MLA_ROPE)
    q_nope = qf[..., :MLA_NOPE]
    q_rope = _rope(qf[..., MLA_NOPE:], pos)
    c_kv = _rmsnorm(ckv, lw["mla_kv_norm_g"])
    k_rope = _rope(kr, pos)
    q_lat = jnp.einsum("blhd,rhd->blhr", q_nope, lw["mla_w_uk"])
    kv_all = jnp.concatenate([kv_past, c_kv], axis=1)
    kr_all = jnp.concatenate([kr_past, k_rope], axis=1)
    o_lat = _mla_attend(q_lat, q_rope, kv_all, kr_all, t_past)
    y_mla = jnp.einsum("blhr,rhd->blhd", o_lat, lw["mla_w_uv"]).reshape(bsz, l, MLA_WIDTH)

    mix = jnp.concatenate([y_ssd, y_gla, y_mla], axis=-1) @ lw["w_out"]
    x = _layernorm(DN_ALPHA * x + mix, lw["ln1_g"], lw["ln1_b"])

    a, b = jnp.split(x @ lw["ffn_w_up"], 2, axis=-1)
    full_f = jnp.concatenate([ffn_buf0, a], axis=1)
    ffn_buf = full_f[:, full_f.shape[1] - (FFN_CONV - 1):]
    a = _causal_dwconv(full_f, lw["ffn_conv_w"], lw["ffn_conv_b"])
    f = (jax.nn.gelu(a, approximate=False) * b) @ lw["ffn_w_down"]
    ple = jax.nn.sigmoid(x @ lw["ple_w_gate"]) * (pl @ lw["ple_w_proj"])
    x = _layernorm(DN_ALPHA * x + f + ple, lw["ln2_g"], lw["ln2_b"])
    return x, (c_kv, k_rope, ssm_h, ssm_buf, gla_s, ffn_buf)


def setup_inputs(seed: int = 0) -> dict:
    key = jax.random.key(seed)
    ks = iter(jax.random.split(key, 48))

    def nrm(shape, scale=1.0):
        return jax.random.normal(next(ks), shape, jnp.float32) * scale

    def gain(shape):
        return 1.0 + nrm(shape, 0.02)

    n_pages = PAST_LEN // PAGE_SIZE
    n_used = DEC_BATCH * n_pages
    n_pool = (n_used * 5) // 4
    page_table = jax.random.permutation(next(ks), n_pool)[:n_used].reshape(DEC_BATCH, n_pages).astype(jnp.int32)
    dt_init = jnp.exp(jax.random.uniform(next(ks), (DEPTH, SSD_HEADS)) * (math.log(0.1) - math.log(0.001)) + math.log(0.001))
    dt_bias = dt_init + jnp.log(-jnp.expm1(-dt_init))
    a_log = jnp.log(jax.random.uniform(next(ks), (DEPTH, SSD_HEADS), minval=1.0, maxval=16.0))
    return {
        "x_prompt": nrm((BATCH, SEQ, D_MODEL)),
        "x_sample": nrm((DEC_BATCH, DEC_SEQ, D_MODEL)),
        "cache_kv_latent": nrm((DEPTH, n_pool, PAGE_SIZE, MLA_KV_LORA)),
        "cache_k_rope": nrm((DEPTH, n_pool, PAGE_SIZE, MLA_ROPE)),
        "state_ssm": nrm((DEPTH, DEC_BATCH, SSD_HEADS, SSD_HEAD_DIM, SSD_STATE), 0.5),
        "state_ssm_conv": nrm((DEPTH, DEC_BATCH, SSD_CONV - 1, SSD_CONV_CH)),
        "state_gla": nrm((DEPTH, DEC_BATCH, GLA_HEADS, GLA_DK, GLA_DV)),
        "state_ffn_conv": nrm((DEPTH, DEC_BATCH, FFN_CONV - 1, D_FF)),
        "page_table": page_table,
        "p_prompt": nrm((DEPTH, BATCH, SEQ, PLE_DIM)),
        "p_sample": nrm((DEPTH, DEC_BATCH, DEC_SEQ, PLE_DIM)),
        "ln_in_g": gain((D_MODEL,)),
        "ln_in_b": nrm((D_MODEL,), 0.02),
        "w_in": nrm((DEPTH, D_MODEL, IN_COLS), D_MODEL ** -0.5),
        "ssd_conv_w": nrm((DEPTH, SSD_CONV, SSD_CONV_CH), SSD_CONV ** -0.5),
        "ssd_conv_b": nrm((DEPTH, SSD_CONV_CH), 0.02),
        "ssd_dt_bias": dt_bias,
        "ssd_a_log": a_log,
        "ssd_d": gain((DEPTH, SSD_HEADS)),
        "ssd_norm_g": gain((DEPTH, SSD_WIDTH)),
        "gla_gate_w": nrm((DEPTH, GLA_GATE_RANK, GLA_HEADS * GLA_DK), GLA_GATE_RANK ** -0.5),
        "gla_gate_b": nrm((DEPTH, GLA_HEADS * GLA_DK), 0.02),
        "gla_norm_g": gain((DEPTH, GLA_DV)),
        "mla_q_norm_g": gain((DEPTH, MLA_Q_LORA)),
        "mla_w_uq": nrm((DEPTH, MLA_Q_LORA, MLA_HEADS * (MLA_NOPE + MLA_ROPE)), MLA_Q_LORA ** -0.5),
        "mla_kv_norm_g": gain((DEPTH, MLA_KV_LORA)),
        "mla_w_uk": nrm((DEPTH, MLA_KV_LORA, MLA_HEADS, MLA_NOPE), MLA_KV_LORA ** -0.5),
        "mla_w_uv": nrm((DEPTH, MLA_KV_LORA, MLA_HEADS, MLA_DV), MLA_KV_LORA ** -0.5),
        "w_out": nrm((DEPTH, D_MODEL, D_MODEL), D_MODEL ** -0.5 * DN_BETA),
        "ln1_g": gain((DEPTH, D_MODEL)),
        "ln1_b": nrm((DEPTH, D_MODEL), 0.02),
        "ffn_w_up": nrm((DEPTH, D_MODEL, 2 * D_FF), D_MODEL ** -0.5),
        "ffn_conv_w": nrm((DEPTH, FFN_CONV, D_FF), FFN_CONV ** -0.5),
        "ffn_conv_b": nrm((DEPTH, D_FF), 0.02),
        "ffn_w_down": nrm((DEPTH, D_FF, D_MODEL), D_FF ** -0.5 * DN_BETA),
        "ple_w_proj": nrm((DEPTH, PLE_DIM, D_MODEL), PLE_DIM ** -0.5),
        "ple_w_gate": nrm((DEPTH, D_MODEL, D_MODEL), D_MODEL ** -0.5),
        "ln2_g": gain((DEPTH, D_MODEL)),
        "ln2_b": nrm((DEPTH, D_MODEL), 0.02),
    }


def reference(x_prompt, x_sample, cache_kv_latent, cache_k_rope, state_ssm, state_ssm_conv, state_gla,
              state_ffn_conv, page_table, p_prompt, p_sample, ln_in_g, ln_in_b, w_in, ssd_conv_w, ssd_conv_b,
              ssd_dt_bias, ssd_a_log, ssd_d, ssd_norm_g, gla_gate_w, gla_gate_b, gla_norm_g, mla_q_norm_g,
              mla_w_uq, mla_kv_norm_g, mla_w_uk, mla_w_uv, w_out, ln1_g, ln1_b, ffn_w_up, ffn_conv_w,
              ffn_conv_b, ffn_w_down, ple_w_proj, ple_w_gate, ln2_g, ln2_b):
    dtype = x_prompt.dtype
    bp = x_prompt.shape[0]
    bs = x_sample.shape[0]
    hp = _layernorm(x_prompt, ln_in_g, ln_in_b)
    hs = _layernorm(x_sample, ln_in_g, ln_in_b)
    ssm0 = jnp.zeros((bp, SSD_HEADS, SSD_HEAD_DIM, SSD_STATE), dtype)
    ssm_conv0 = jnp.zeros((bp, SSD_CONV - 1, SSD_CONV_CH), dtype)
    gla0 = jnp.zeros((bp, GLA_HEADS, GLA_DK, GLA_DV), dtype)
    ffn_conv0 = jnp.zeros((bp, FFN_CONV - 1, D_FF), dtype)
    kv0 = jnp.zeros((bp, 0, MLA_KV_LORA), dtype)
    kr0 = jnp.zeros((bp, 0, MLA_ROPE), dtype)
    prompt_states = []
    sample_states = []
    for i in range(DEPTH):
        lw = {
            "w_in": w_in[i], "ssd_conv_w": ssd_conv_w[i], "ssd_conv_b": ssd_conv_b[i],
            "ssd_dt_bias": ssd_dt_bias[i], "ssd_a_log": ssd_a_log[i], "ssd_d": ssd_d[i],
            "ssd_norm_g": ssd_norm_g[i], "gla_gate_w": gla_gate_w[i], "gla_gate_b": gla_gate_b[i],
            "gla_norm_g": gla_norm_g[i], "mla_q_norm_g": mla_q_norm_g[i], "mla_w_uq": mla_w_uq[i],
            "mla_kv_norm_g": mla_kv_norm_g[i], "mla_w_uk": mla_w_uk[i], "mla_w_uv": mla_w_uv[i],
            "w_out": w_out[i], "ln1_g": ln1_g[i], "ln1_b": ln1_b[i], "ffn_w_up": ffn_w_up[i],
            "ffn_conv_w": ffn_conv_w[i], "ffn_conv_b": ffn_conv_b[i], "ffn_w_down": ffn_w_down[i],
            "ple_w_proj": ple_w_proj[i], "ple_w_gate": ple_w_gate[i], "ln2_g": ln2_g[i], "ln2_b": ln2_b[i],
        }
        hp, sp = _layer(hp, p_prompt[i], ssm0, ssm_conv0, gla0, ffn_conv0, kv0, kr0, lw)
        kv_past = cache_kv_latent[i][page_table].reshape(bs, -1, MLA_KV_LORA)
        kr_past = cache_k_rope[i][page_table].reshape(bs, -1, MLA_ROPE)
        hs, ss = _layer(hs, p_sample[i], state_ssm[i], state_ssm_conv[i], state_gla[i], state_ffn_conv[i],
                        kv_past, kr_past, lw)
        prompt_states.append(sp)
        sample_states.append(ss)
    kv_p, kr_p, ssm_p, ssm_conv_p, gla_p, ffn_conv_p = (jnp.stack(f) for f in zip(*prompt_states))
    kv_s, kr_s, ssm_s, ssm_conv_s, gla_s, ffn_conv_s = (jnp.stack(f) for f in zip(*sample_states))
    return (hp, hs, kv_p, kv_s, kr_p, kr_s, ssm_p, ssm_s, ssm_conv_p, ssm_conv_s, gla_p, gla_s, ffn_conv_p, ffn_conv_s)
```

```python
import functools
import math

import jax
import jax.numpy as jnp
import numpy as np
from jax import lax
from jax.experimental import pallas as pl
from jax.experimental.pallas import tpu as pltpu

f32 = jnp.float32
bf16 = jnp.bfloat16

D_MODEL = 1024
HEAD_DIM = 64
SSD_WIDTH = 512
SSD_HEADS = 8
SSD_GROUPS = 2
SSD_STATE = 128
SSD_CONV = 4
SSD_CONV_CH = SSD_WIDTH + 2 * SSD_GROUPS * SSD_STATE
GLA_WIDTH = 256
GLA_HEADS = 4
GLA_DK = 32
GLA_DV = 64
GLA_GATE_RANK = 16
GLA_TAU = 16.0
MLA_HEADS = 4
MLA_NOPE = 64
MLA_ROPE = 32
MLA_DV = 64
MLA_Q_LORA = 192
MLA_KV_LORA = 128
MLA_SCALE = (MLA_NOPE + MLA_ROPE) ** -0.5
ROPE_THETA = 10000.0
D_FF = 2816
FFN_CONV = 3
PLE_DIM = 256
PAGE_SIZE = 128
DEPTH = 4
DN_ALPHA = (2 * DEPTH) ** 0.25

P_SSD = 1664
P_GLA = 896
P_MLA = 640
QK_W = 256

SEG_PAD = 8
FF_CHUNK = 256
PAGES_PER_STEP = 16
VMEM_LIMIT = 56 * 1024 * 1024

_HI = lax.Precision.HIGHEST


def _bdot(a, b):
    return jnp.dot(a.astype(bf16), b.astype(bf16), preferred_element_type=f32)


def _bdot_nt(a, b):
    return lax.dot_general(a.astype(bf16), b.astype(bf16), (((1,), (1,)), ((), ())),
                           preferred_element_type=f32)


def _bdot_tn(a, b):
    return lax.dot_general(a.astype(bf16), b.astype(bf16), (((0,), (0,)), ((), ())),
                           preferred_element_type=f32)


def _hdot(a, b):
    return jnp.dot(a, b, precision=_HI, preferred_element_type=f32)


def _silu(x):
    return x * jax.nn.sigmoid(x)


def _softplus(x):
    return jnp.maximum(x, 0.0) + jnp.log1p(jnp.exp(-jnp.abs(x)))


def _log_sigmoid(x):
    return jnp.minimum(x, 0.0) - jnp.log1p(jnp.exp(-jnp.abs(x)))


def _layernorm(x, g, b, eps=1e-5):
    xc = x - jnp.mean(x, -1, keepdims=True)
    var = jnp.mean(xc * xc, -1, keepdims=True)
    return xc * lax.rsqrt(var + eps) * g + b


def _tri(c):
    r = lax.broadcasted_iota(jnp.int32, (c, c), 0)
    col = lax.broadcasted_iota(jnp.int32, (c, c), 1)
    return col <= r


def _params(sem):
    return pltpu.CompilerParams(dimension_semantics=sem, vmem_limit_bytes=VMEM_LIMIT)


def _const_spec(shape):
    nd = len(shape)
    return pl.BlockSpec(shape, lambda *_: (0,) * nd)


def _ln_kernel(x_ref, g_ref, b_ref, o_ref):
    o_ref[...] = _layernorm(x_ref[...], g_ref[...], b_ref[...])


def _ln_in(x2, g, b, tm):
    rows = x2.shape[0]
    return pl.pallas_call(
        _ln_kernel,
        grid=(rows // tm,),
        in_specs=[pl.BlockSpec((tm, D_MODEL), lambda i: (i, 0)), _const_spec((1, D_MODEL)), _const_spec((1, D_MODEL))],
        out_specs=pl.BlockSpec((tm, D_MODEL), lambda i: (i, 0)),
        out_shape=jax.ShapeDtypeStruct((rows, D_MODEL), f32),
        compiler_params=_params(("parallel",)),
        name="ln_in",
    )(x2, g, b)


def _in_proj_kernel(x_ref, w_ref, ssd_ref, gla_ref, mla_ref):
    x = x_ref[...].astype(bf16)
    ssd_ref[...] = jnp.dot(x, w_ref[:, 0:P_SSD], preferred_element_type=f32)
    gla_ref[...] = jnp.dot(x, w_ref[:, P_SSD:P_SSD + P_GLA], preferred_element_type=f32)
    mla_ref[...] = jnp.dot(x, w_ref[:, P_SSD + P_GLA:], preferred_element_type=f32)


def _in_proj(x2, w, tm):
    rows = x2.shape[0]
    wtot = P_SSD + P_GLA + P_MLA
    return pl.pallas_call(
        _in_proj_kernel,
        grid=(rows // tm,),
        in_specs=[pl.BlockSpec((tm, D_MODEL), lambda i: (i, 0)), _const_spec((D_MODEL, wtot))],
        out_specs=[pl.BlockSpec((tm, P_SSD), lambda i: (i, 0)),
                   pl.BlockSpec((tm, P_GLA), lambda i: (i, 0)),
                   pl.BlockSpec((tm, P_MLA), lambda i: (i, 0))],
        out_shape=[jax.ShapeDtypeStruct((rows, P_SSD), f32),
                   jax.ShapeDtypeStruct((rows, P_GLA), f32),
                   jax.ShapeDtypeStruct((rows, P_MLA), f32)],
        compiler_params=_params(("parallel",)),
        name="in_proj",
    )(x2, w)


def _ssd_kernel(c, valid, p_ref, halo0_ref, h0_ref, cw_ref, cb_ref, dtb_ref, alog_ref, dexp_ref, ng_ref,
                y_ref, hout_ref, halo_out_ref, ht_scr, halo_scr):
    ci = pl.program_id(1)
    hp = SSD_HEADS * HEAD_DIM
    gw = hp // SSD_GROUPS

    @pl.when(ci == 0)
    def _():
        ht_scr[...] = h0_ref[0].reshape(hp, SSD_STATE).T
        halo_scr[...] = halo0_ref[0]

    p = p_ref[0]
    z = p[:, 0:SSD_WIDTH]
    xr = p[:, SSD_WIDTH:SSD_WIDTH + SSD_CONV_CH]
    dt_raw = p[:, SSD_WIDTH + SSD_CONV_CH:SSD_WIDTH + SSD_CONV_CH + SSD_HEADS]

    halo = halo_scr[...]
    row = lax.broadcasted_iota(jnp.int32, (c, 1), 0)
    hm1, hm2, hm3 = halo[7:8, :], halo[6:7, :], halo[5:6, :]
    s1 = jnp.where(row >= 1, pltpu.roll(xr, 1, axis=0), hm1)
    s2 = jnp.where(row >= 2, pltpu.roll(xr, 2, axis=0), jnp.where(row == 1, hm1, hm2))
    s3 = jnp.where(row >= 3, pltpu.roll(xr, 3, axis=0),
                   jnp.where(row == 2, hm1, jnp.where(row == 1, hm2, hm3)))
    cw = cw_ref[...]
    conv = cw[3:4, :] * xr + cw[2:3, :] * s1 + cw[1:2, :] * s2 + cw[0:1, :] * s3 + cb_ref[...]
    last = xr[c - 8:c, :]
    if valid < c:
        last = pltpu.roll(last, c - valid, axis=0)
    halo_scr[...] = last

    xbc = _silu(conv)
    xs = xbc[:, 0:SSD_WIDTH]
    bs = xbc[:, SSD_WIDTH:SSD_WIDTH + SSD_GROUPS * SSD_STATE]
    cs = xbc[:, SSD_WIDTH + SSD_GROUPS * SSD_STATE:]

    dt = _softplus(dt_raw + dtb_ref[...])
    if valid < c:
        dt = jnp.where(row < valid, dt, 0.0)
    la = dt * (-jnp.exp(alog_ref[...]))
    tri = _tri(c)
    cum = _hdot(tri.astype(f32), la)
    cum_t = cum.T
    dt_t = dt.T
    ecum = jnp.exp(cum)
    w = jnp.exp(cum[c - 1:c, :] - cum) * dt
    hl = lax.broadcasted_iota(jnp.int32, (SSD_HEADS, hp), 1) // HEAD_DIM
    hh = lax.broadcasted_iota(jnp.int32, (SSD_HEADS, hp), 0)
    expand = (hl == hh).astype(f32)
    ecum_x = _hdot(ecum, expand)
    w_x = _hdot(w, expand)

    ht = ht_scr[...]
    lane_head = lax.broadcasted_iota(jnp.int32, (1, gw), 1) // HEAD_DIM
    heads_per_group = SSD_HEADS // SSD_GROUPS
    y_parts = []
    for g in range(SSD_GROUPS):
        sl = slice(g * gw, (g + 1) * gw)
        b_g = bs[:, g * SSD_STATE:(g + 1) * SSD_STATE]
        c_g = cs[:, g * SSD_STATE:(g + 1) * SSD_STATE]
        x_g = xs[:, sl]
        scores = _bdot_nt(c_g, b_g)
        y_g = _bdot(c_g, ht[:, sl]) * ecum_x[:, sl]
        for hh_ in range(heads_per_group):
            h = g * heads_per_group + hh_
            seg = cum[:, h:h + 1] - cum_t[h:h + 1, :]
            decay = jnp.exp(jnp.where(tri, seg, -jnp.inf))
            m = scores * decay * dt_t[h:h + 1, :]
            xm = jnp.where(lane_head == hh_, x_g, 0.0)
            y_g = y_g + _bdot(m, xm)
        y_parts.append(y_g)
        ht_scr[:, sl] = ht[:, sl] * ecum_x[c - 1:c, sl] + _bdot_tn(b_g, x_g * w_x[:, sl])
    y = jnp.concatenate(y_parts, axis=-1)
    y = (y + dexp_ref[...] * xs) * _silu(z)
    ng = ng_ref[...]
    outs = []
    for g in range(SSD_GROUPS):
        sl = slice(g * gw, (g + 1) * gw)
        yg = y[:, sl]
        outs.append(yg * lax.rsqrt(jnp.mean(yg * yg, -1, keepdims=True) + 1e-6) * ng[:, sl])
    y_ref[0] = jnp.concatenate(outs, axis=-1).astype(bf16)

    @pl.when(ci == pl.num_programs(1) - 1)
    def _():
        hout_ref[0] = ht_scr[...].T.reshape(SSD_HEADS, HEAD_DIM, SSD_STATE)
        halo_out_ref[0] = halo_scr[...]


def _ssd(p_ssd, halo0, h0, lw, c, valid):
    nseq, lp, _ = p_ssd.shape
    hp = SSD_HEADS * HEAD_DIM
    return pl.pallas_call(
        functools.partial(_ssd_kernel, c, valid),
        grid=(nseq, lp // c),
        in_specs=[pl.BlockSpec((1, c, P_SSD), lambda b, i: (b, i, 0)),
                  pl.BlockSpec((1, 8, SSD_CONV_CH), lambda b, i: (b, 0, 0)),
                  pl.BlockSpec((1, SSD_HEADS, HEAD_DIM, SSD_STATE), lambda b, i: (b, 0, 0, 0)),
                  _const_spec((SSD_CONV, SSD_CONV_CH)), _const_spec((1, SSD_CONV_CH)),
                  _const_spec((1, SSD_HEADS)), _const_spec((1, SSD_HEADS)),
                  _const_spec((1, hp)), _const_spec((1, hp))],
        out_specs=[pl.BlockSpec((1, c, hp), lambda b, i: (b, i, 0)),
                   pl.BlockSpec((1, SSD_HEADS, HEAD_DIM, SSD_STATE), lambda b, i: (b, 0, 0, 0)),
                   pl.BlockSpec((1, 8, SSD_CONV_CH), lambda b, i: (b, 0, 0))],
        out_shape=[jax.ShapeDtypeStruct((nseq, lp, hp), bf16),
                   jax.ShapeDtypeStruct((nseq, SSD_HEADS, HEAD_DIM, SSD_STATE), f32),
                   jax.ShapeDtypeStruct((nseq, 8, SSD_CONV_CH), f32)],
        scratch_shapes=[pltpu.VMEM((SSD_STATE, hp), f32), pltpu.VMEM((8, SSD_CONV_CH), f32)],
        compiler_params=_params(("parallel", "arbitrary")),
        name="ssd",
    )(p_ssd, halo0, h0, lw["ssd_conv_w"], lw["ssd_conv_b"], lw["ssd_dt_bias"], lw["ssd_a_log"],
      lw["ssd_d_x"], lw["ssd_norm_g"])


def _gla_kernel(c, valid, p_ref, st0_ref, gw_ref, gb_ref, gn_ref, y_ref, st_out_ref, st_scr):
    ci = pl.program_id(1)
    hk = GLA_HEADS * GLA_DK
    hv = GLA_HEADS * GLA_DV

    @pl.when(ci == 0)
    def _():
        st_scr[...] = st0_ref[0]

    p = p_ref[0]
    q = p[:, 0:hk] * GLA_DK ** -0.5
    k = p[:, hk:2 * hk]
    v = p[:, 2 * hk:2 * hk + hv]
    r = p[:, 2 * hk + hv:2 * hk + 2 * hv]
    glr = p[:, 2 * hk + 2 * hv:2 * hk + 2 * hv + GLA_GATE_RANK]
    log_a = _log_sigmoid(_hdot(glr, gw_ref[...]) + gb_ref[...]) / GLA_TAU
    if valid < c:
        row = lax.broadcasted_iota(jnp.int32, (c, 1), 0)
        log_a = jnp.where(row < valid, log_a, 0.0)
        k = jnp.where(row < valid, k, 0.0)
    tri = _tri(c)
    b = _hdot(tri.astype(f32), log_a)
    b_last = b[c - 1:c, :]
    qt = q * jnp.exp(b)
    kt = k * jnp.exp(-b)
    khat = k * jnp.exp(b_last - b)

    st = st_scr[...]
    o = _bdot_nt(qt, st)
    lane_k = lax.broadcasted_iota(jnp.int32, (1, hk), 1) // GLA_DK
    lane_v = lax.broadcasted_iota(jnp.int32, (1, hv), 1) // GLA_DV
    for h in range(GLA_HEADS):
        att = _bdot_nt(jnp.where(lane_k == h, qt, 0.0), kt)
        att = jnp.where(tri, att, 0.0)
        o = o + _bdot(att, jnp.where(lane_v == h, v, 0.0))
    blk = (lax.broadcasted_iota(jnp.int32, (hv, hk), 0) // GLA_DV
           == lax.broadcasted_iota(jnp.int32, (hv, hk), 1) // GLA_DK)
    st_scr[...] = jnp.where(blk, st * jnp.exp(b_last) + _bdot_tn(v, khat), 0.0)

    oo = o * o
    oo_hi = oo.astype(bf16)
    oo_lo = (oo - oo_hi.astype(f32)).astype(bf16)
    grp = (lax.broadcasted_iota(jnp.int32, (hv, hv), 0) // GLA_DV
           == lax.broadcasted_iota(jnp.int32, (hv, hv), 1) // GLA_DV)
    gmat = jnp.where(grp, 1.0 / GLA_DV, 0.0).astype(bf16)
    ms = jnp.dot(oo_hi, gmat, preferred_element_type=f32) + jnp.dot(oo_lo, gmat, preferred_element_type=f32)
    y = o * lax.rsqrt(ms + 1e-6) * gn_ref[...] * _silu(r)
    y_ref[0] = y.astype(bf16)

    @pl.when(ci == pl.num_programs(1) - 1)
    def _():
        st_out_ref[0] = st_scr[...]


def _gla(p_gla, st0, lw, c, valid):
    nseq, lp, _ = p_gla.shape
    hk = GLA_HEADS * GLA_DK
    hv = GLA_HEADS * GLA_DV
    return pl.pallas_call(
        functools.partial(_gla_kernel, c, valid),
        grid=(nseq, lp // c),
        in_specs=[pl.BlockSpec((1, c, P_GLA), lambda b, i: (b, i, 0)),
                  pl.BlockSpec((1, hv, hk), lambda b, i: (b, 0, 0)),
                  _const_spec((GLA_GATE_RANK, hk)), _const_spec((1, hk)), _const_spec((1, hv))],
        out_specs=[pl.BlockSpec((1, c, hv), lambda b, i: (b, i, 0)),
                   pl.BlockSpec((1, hv, hk), lambda b, i: (b, 0, 0))],
        out_shape=[jax.ShapeDtypeStruct((nseq, lp, hv), bf16),
                   jax.ShapeDtypeStruct((nseq, hv, hk), f32)],
        scratch_shapes=[pltpu.VMEM((hv, hk), f32)],
        compiler_params=_params(("parallel", "arbitrary")),
        name="gla",
    )(p_gla, st0, lw["gla_gate_w"], lw["gla_gate_b"], lw["gla_norm_g_x"])


def _mla_prep_kernel(nsb, ts, p_ref, cos_ref, sin_ref, qg_ref, kvg_ref, wuq_ref, wcat_ref, sel_ref,
                     qcat_ref, kcat_ref, ckv_ref, krope_ref):
    rows = nsb * ts
    p = p_ref[...].reshape(rows, P_MLA)
    cq = p[:, 0:256]
    ckv = p[:, 256:384]
    kra = p[:, 384:512]
    krb = p[:, 512:640]
    cos = cos_ref[...]
    sin = sin_ref[...]
    cqn = cq * lax.rsqrt(jnp.sum(cq * cq, -1, keepdims=True) * (1.0 / MLA_Q_LORA) + 1e-6) * qg_ref[...]
    qf = _bdot(cqn, wuq_ref[...])
    ra = qf[:, 256:384]
    rb = qf[:, 384:512]
    feat = jnp.concatenate([qf[:, 0:256], ra * cos - rb * sin, ra * sin + rb * cos], axis=-1).astype(bf16)
    for h in range(MLA_HEADS):
        qh = jnp.dot(feat, wcat_ref[h], preferred_element_type=f32).astype(bf16)
        qcat_ref[:, h] = qh.reshape(nsb, ts, QK_W)
    c_kv = ckv * lax.rsqrt(jnp.mean(ckv * ckv, -1, keepdims=True) + 1e-6) * kvg_ref[...]
    kr2 = jnp.concatenate([kra * cos - krb * sin, kra * sin + krb * cos], axis=-1)
    kr128 = _hdot(kr2, sel_ref[...])
    ckv_ref[...] = c_kv.reshape(nsb, ts, MLA_KV_LORA)
    krope_ref[...] = kr128[:, 0:MLA_ROPE].reshape(nsb, ts, MLA_ROPE)
    kcat_ref[...] = jnp.concatenate([c_kv, kr128], axis=-1).astype(bf16).reshape(nsb, ts, QK_W)


def _mla_prep(p_mla, cos, sin, lw, nsb, ts):
    nseq, lp, _ = p_mla.shape
    rows = nsb * ts
    ntile = lp // ts
    tbl_tiles = cos.shape[0] // rows
    return pl.pallas_call(
        functools.partial(_mla_prep_kernel, nsb, ts),
        grid=(nseq // nsb, ntile),
        in_specs=[pl.BlockSpec((nsb, ts, P_MLA), lambda b, i: (b, i, 0)),
                  pl.BlockSpec((rows, 128), lambda b, i: (i % tbl_tiles, 0)),
                  pl.BlockSpec((rows, 128), lambda b, i: (i % tbl_tiles, 0)),
                  _const_spec((1, 256)), _const_spec((1, MLA_KV_LORA)),
                  _const_spec((256, 512)), _const_spec((MLA_HEADS, 512, QK_W)), _const_spec((256, 128))],
        out_specs=[pl.BlockSpec((nsb, MLA_HEADS, ts, QK_W), lambda b, i: (b, 0, i, 0)),
                   pl.BlockSpec((nsb, ts, QK_W), lambda b, i: (b, i, 0)),
                   pl.BlockSpec((nsb, ts, MLA_KV_LORA), lambda b, i: (b, i, 0)),
                   pl.BlockSpec((nsb, ts, MLA_ROPE), lambda b, i: (b, i, 0))],
        out_shape=[jax.ShapeDtypeStruct((nseq, MLA_HEADS, lp, QK_W), bf16),
                   jax.ShapeDtypeStruct((nseq, lp, QK_W), bf16),
                   jax.ShapeDtypeStruct((nseq, lp, MLA_KV_LORA), f32),
                   jax.ShapeDtypeStruct((nseq, lp, MLA_ROPE), f32)],
        compiler_params=_params(("parallel", "parallel")),
        name="mla_prep",
    )(p_mla, cos, sin, lw["mla_q_norm_g_x"], lw["mla_kv_norm_g"], lw["mla_w_uq_x"], lw["mla_wcat"],
      lw["mla_sel"])


def _mla_flash_kernel(tq, tk, q_ref, k_ref, wuv_ref, y_ref, m_scr, l_scr, acc_scr):
    qi = pl.program_id(1)
    ki = pl.program_id(2)
    m_rows = MLA_HEADS * tq

    @pl.when(ki == 0)
    def _():
        m_scr[...] = jnp.full(m_scr.shape, -jnp.inf, f32)
        l_scr[...] = jnp.zeros(l_scr.shape, f32)
        acc_scr[...] = jnp.zeros(acc_scr.shape, f32)

    def step(masked):
        q = q_ref[0].reshape(m_rows, QK_W)
        k = k_ref[0]
        s = lax.dot_general(q, k, (((1,), (1,)), ((), ())), preferred_element_type=f32) * MLA_SCALE
        if masked:
            rpos = lax.broadcasted_iota(jnp.int32, (MLA_HEADS, tq, tk), 1).reshape(m_rows, tk)
            cpos = lax.broadcasted_iota(jnp.int32, (m_rows, tk), 1)
            s = jnp.where(cpos <= rpos, s, -jnp.inf)
        m_prev = m_scr[...]
        m_new = jnp.maximum(m_prev, jnp.max(s, -1, keepdims=True))
        alpha = jnp.exp(m_prev - m_new)
        pr = jnp.exp(s - m_new)
        l_scr[...] = alpha * l_scr[...] + jnp.sum(pr, -1, keepdims=True)
        acc_scr[...] = alpha * acc_scr[...] + jnp.dot(pr.astype(bf16), k, preferred_element_type=f32)
        m_scr[...] = m_new

    @pl.when(ki < qi)
    def _():
        step(False)

    @pl.when(ki == qi)
    def _():
        step(True)
        o_lat = acc_scr[:, 0:MLA_KV_LORA] / l_scr[...]
        y = jnp.zeros((tq, MLA_HEADS * MLA_DV), f32)
        for h in range(MLA_HEADS):
            y = y + _bdot(o_lat[h * tq:(h + 1) * tq, :], wuv_ref[h])
        y_ref[0] = y.astype(bf16)


def _mla_flash(qcat, kcat, lw, tq):
    nseq, _, lp, _ = qcat.shape
    tk = tq
    nq = lp // tq
    return pl.pallas_call(
        functools.partial(_mla_flash_kernel, tq, tk),
        grid=(nseq, nq, nq),
        in_specs=[pl.BlockSpec((1, MLA_HEADS, tq, QK_W), lambda b, i, j: (b, 0, i, 0)),
                  pl.BlockSpec((1, tk, QK_W), lambda b, i, j: (b, jnp.minimum(i, j), 0)),
                  _const_spec((MLA_HEADS, MLA_KV_LORA, MLA_HEADS * MLA_DV))],
        out_specs=pl.BlockSpec((1, tq, MLA_HEADS * MLA_DV), lambda b, i, j: (b, i, 0)),
        out_shape=jax.ShapeDtypeStruct((nseq, lp, MLA_HEADS * MLA_DV), bf16),
        scratch_shapes=[pltpu.VMEM((MLA_HEADS * tq, 1), f32), pltpu.VMEM((MLA_HEADS * tq, 1), f32),
                        pltpu.VMEM((MLA_HEADS * tq, QK_W), f32)],
        compiler_params=_params(("parallel", "parallel", "arbitrary")),
        name="mla_flash",
    )(qcat, kcat, lw["mla_wuv_x"])


def _mla_paged_kernel(valid, npage, pt_ref, q_ref, knew_ref, wuv_ref, *rest):
    kv_refs = rest[:npage]
    kr_refs = rest[npage:2 * npage]
    y_ref, m_scr, l_scr, acc_scr = rest[2 * npage:]
    j = pl.program_id(1)
    m_rows = MLA_HEADS * SEG_PAD

    @pl.when(j == 0)
    def _():
        m_scr[...] = jnp.full(m_scr.shape, -jnp.inf, f32)
        l_scr[...] = jnp.zeros(l_scr.shape, f32)
        acc_scr[...] = jnp.zeros(acc_scr.shape, f32)

    q = q_ref[0].reshape(m_rows, QK_W)
    q_lat = q[:, 0:MLA_KV_LORA]
    q_rope = q[:, MLA_KV_LORA:MLA_KV_LORA + MLA_ROPE]
    kvs = [kv_refs[i][0, 0].astype(bf16) for i in range(npage)]
    s_parts = []
    for i in range(npage):
        kr = kr_refs[i][0, 0].astype(bf16)
        s_parts.append(lax.dot_general(q_lat, kvs[i], (((1,), (1,)), ((), ())), preferred_element_type=f32)
                       + lax.dot_general(q_rope, kr, (((1,), (1,)), ((), ())), preferred_element_type=f32))
    s = jnp.concatenate(s_parts, axis=-1) * MLA_SCALE
    m_prev = m_scr[...]
    m_new = jnp.maximum(m_prev, jnp.max(s, -1, keepdims=True))
    alpha = jnp.exp(m_prev - m_new)
    pr = jnp.exp(s - m_new)
    l_new = alpha * l_scr[...] + jnp.sum(pr, -1, keepdims=True)
    acc = alpha * acc_scr[...]
    prb = pr.astype(bf16)
    for i in range(npage):
        acc = acc + jnp.dot(prb[:, i * PAGE_SIZE:(i + 1) * PAGE_SIZE], kvs[i], preferred_element_type=f32)
    m_scr[...] = m_new
    l_scr[...] = l_new
    acc_scr[...] = acc

    @pl.when(j == pl.num_programs(1) - 1)
    def _():
        knew = knew_ref[0]
        sn = lax.dot_general(q, knew, (((1,), (1,)), ((), ())), preferred_element_type=f32) * MLA_SCALE
        rpos = lax.broadcasted_iota(jnp.int32, (MLA_HEADS, SEG_PAD, SEG_PAD), 1).reshape(m_rows, SEG_PAD)
        cpos = lax.broadcasted_iota(jnp.int32, (m_rows, SEG_PAD), 1)
        sn = jnp.where((cpos <= rpos) & (cpos < valid), sn, -jnp.inf)
        m_fin = jnp.maximum(m_new, jnp.max(sn, -1, keepdims=True))
        a2 = jnp.exp(m_new - m_fin)
        pn = jnp.exp(sn - m_fin)
        l_fin = a2 * l_new + jnp.sum(pn, -1, keepdims=True)
        acc_fin = a2 * acc + jnp.dot(pn.astype(bf16), knew[:, 0:MLA_KV_LORA], preferred_element_type=f32)
        o_lat = acc_fin / l_fin
        y = jnp.zeros((SEG_PAD, MLA_HEADS * MLA_DV), f32)
        for h in range(MLA_HEADS):
            y = y + _bdot(o_lat[h * SEG_PAD:(h + 1) * SEG_PAD, :], wuv_ref[h])
        y_ref[0] = y.astype(bf16)


def _mla_paged(qcat, kcat, cache_kv, cache_kr, page_table, layer, lw, valid):
    nseq = qcat.shape[0]
    n_pages = page_table.shape[1]
    npage = math.gcd(PAGES_PER_STEP, n_pages)
    nsteps = n_pages // npage

    def kv_map(i):
        return lambda b, j, pt: (layer, pt[b, j * npage + i], 0, 0)

    kv_specs = [pl.BlockSpec((1, 1, PAGE_SIZE, MLA_KV_LORA), kv_map(i)) for i in range(npage)]
    kr_specs = [pl.BlockSpec((1, 1, PAGE_SIZE, MLA_ROPE), kv_map(i)) for i in range(npage)]
    grid_spec = pltpu.PrefetchScalarGridSpec(
        num_scalar_prefetch=1,
        grid=(nseq, nsteps),
        in_specs=[pl.BlockSpec((1, MLA_HEADS, SEG_PAD, QK_W), lambda b, j, pt: (b, 0, 0, 0)),
                  pl.BlockSpec((1, SEG_PAD, QK_W), lambda b, j, pt: (b, 0, 0)),
                  pl.BlockSpec((MLA_HEADS, MLA_KV_LORA, MLA_HEADS * MLA_DV), lambda b, j, pt: (0, 0, 0))]
        + kv_specs + kr_specs,
        out_specs=pl.BlockSpec((1, SEG_PAD, MLA_HEADS * MLA_DV), lambda b, j, pt: (b, 0, 0)),
        scratch_shapes=[pltpu.VMEM((MLA_HEADS * SEG_PAD, 1), f32), pltpu.VMEM((MLA_HEADS * SEG_PAD, 1), f32),
                        pltpu.VMEM((MLA_HEADS * SEG_PAD, MLA_KV_LORA), f32)],
    )
    return pl.pallas_call(
        functools.partial(_mla_paged_kernel, valid, npage),
        grid_spec=grid_spec,
        out_shape=jax.ShapeDtypeStruct((nseq, SEG_PAD, MLA_HEADS * MLA_DV), bf16),
        compiler_params=_params(("parallel", "arbitrary")),
        name="mla_paged",
    )(page_table, qcat, kcat, lw["mla_wuv_x"], *([cache_kv] * npage), *([cache_kr] * npage))


def _ffn_kernel(nsb, ts, valid, x_ref, ys_ref, yg_ref, ym_ref, pe_ref, halo0_ref,
                wout_ref, g1_ref, b1_ref, wup_ref, fcw_ref, fcb_ref, wdn_ref, wpg_ref, wpp_ref, g2_ref, b2_ref,
                o_ref, halo_out_ref, halo_scr):
    ti = pl.program_id(1)
    rows = nsb * ts

    @pl.when(ti == 0)
    def _():
        halo_scr[...] = halo0_ref[...]

    x = x_ref[...].reshape(rows, D_MODEL)
    mix = (jnp.dot(ys_ref[...].reshape(rows, SSD_WIDTH), wout_ref[0:SSD_WIDTH, :], preferred_element_type=f32)
           + jnp.dot(yg_ref[...].reshape(rows, GLA_WIDTH), wout_ref[SSD_WIDTH:SSD_WIDTH + GLA_WIDTH, :],
                     preferred_element_type=f32)
           + jnp.dot(ym_ref[...].reshape(rows, MLA_HEADS * MLA_DV), wout_ref[SSD_WIDTH + GLA_WIDTH:, :],
                     preferred_element_type=f32))
    x1 = _layernorm(DN_ALPHA * x + mix, g1_ref[...], b1_ref[...])
    x1b = x1.astype(bf16)

    pos = lax.broadcasted_iota(jnp.int32, (nsb, ts, 1), 1)
    f = jnp.zeros((rows, D_MODEL), f32)
    for ch in range(D_FF // FF_CHUNK):
        lo = ch * FF_CHUNK
        a = jnp.dot(x1b, wup_ref[:, lo:lo + FF_CHUNK], preferred_element_type=f32)
        bgate = jnp.dot(x1b, wup_ref[:, D_FF + lo:D_FF + lo + FF_CHUNK], preferred_element_type=f32)
        a3 = a.reshape(nsb, ts, FF_CHUNK)
        halo = halo_scr[:, :, lo:lo + FF_CHUNK]
        hm1, hm2 = halo[:, 7:8, :], halo[:, 6:7, :]
        s1 = jnp.where(pos >= 1, pltpu.roll(a3, 1, axis=1), hm1)
        s2 = jnp.where(pos >= 2, pltpu.roll(a3, 2, axis=1), jnp.where(pos == 1, hm1, hm2))
        fcw = fcw_ref[:, lo:lo + FF_CHUNK]
        ac = fcw[2:3, :] * a3 + fcw[1:2, :] * s1 + fcw[0:1, :] * s2 + fcb_ref[:, lo:lo + FF_CHUNK]
        last = a3[:, ts - 8:ts, :]
        if valid < ts:
            last = pltpu.roll(last, ts - valid, axis=1)
        halo_scr[:, :, lo:lo + FF_CHUNK] = last
        ge = 0.5 * ac * (1.0 + lax.erf(ac * (1.0 / math.sqrt(2.0))))
        hmid = (ge.reshape(rows, FF_CHUNK) * bgate).astype(bf16)
        f = f + jnp.dot(hmid, wdn_ref[lo:lo + FF_CHUNK, :], preferred_element_type=f32)

    gate = jax.nn.sigmoid(jnp.dot(x1b, wpg_ref[...], preferred_element_type=f32))
    pe = _bdot(pe_ref[...].reshape(rows, PLE_DIM), wpp_ref[...])
    x2 = _layernorm(DN_ALPHA * x1 + f + gate * pe, g2_ref[...], b2_ref[...])
    o_ref[...] = x2.reshape(nsb, ts, D_MODEL)

    @pl.when(ti == pl.num_programs(1) - 1)
    def _():
        halo_out_ref[...] = halo_scr[...]


def _ffn(x3, ys, yg, ym, pe, halo0, lw, nsb, ts, valid):
    nseq, lp, _ = x3.shape

    def tok(width):
        return pl.BlockSpec((nsb, ts, width), lambda b, i: (b, i, 0))

    def seq(width):
        return pl.BlockSpec((nsb, 8, width), lambda b, i: (b, 0, 0))

    return pl.pallas_call(
        functools.partial(_ffn_kernel, nsb, ts, valid),
        grid=(nseq // nsb, lp // ts),
        in_specs=[tok(D_MODEL), tok(SSD_WIDTH), tok(GLA_WIDTH), tok(MLA_HEADS * MLA_DV), tok(PLE_DIM), seq(D_FF),
                  _const_spec((D_MODEL, D_MODEL)), _const_spec((1, D_MODEL)), _const_spec((1, D_MODEL)),
                  _const_spec((D_MODEL, 2 * D_FF)), _const_spec((FFN_CONV, D_FF)), _const_spec((1, D_FF)),
                  _const_spec((D_FF, D_MODEL)), _const_spec((D_MODEL, D_MODEL)), _const_spec((PLE_DIM, D_MODEL)),
                  _const_spec((1, D_MODEL)), _const_spec((1, D_MODEL))],
        out_specs=[tok(D_MODEL), seq(D_FF)],
        out_shape=[jax.ShapeDtypeStruct((nseq, lp, D_MODEL), f32),
                   jax.ShapeDtypeStruct((nseq, 8, D_FF), f32)],
        scratch_shapes=[pltpu.VMEM((nsb, 8, D_FF), f32)],
        compiler_params=_params(("parallel", "arbitrary")),
        name="out_ffn",
    )(x3, ys, yg, ym, pe, halo0, lw["w_out"], lw["ln1_g"], lw["ln1_b"], lw["ffn_w_up"], lw["ffn_conv_w"],
      lw["ffn_conv_b"], lw["ffn_w_down"], lw["ple_w_gate"], lw["ple_w_proj"], lw["ln2_g"], lw["ln2_b"])


def _prep_layer_weights(i, w):
    def row(v):
        return v.reshape(1, -1).astype(f32)

    splits = np.cumsum([0, SSD_WIDTH, SSD_CONV_CH, SSD_HEADS, GLA_HEADS * GLA_DK, GLA_HEADS * GLA_DK, GLA_WIDTH,
                        GLA_WIDTH, GLA_GATE_RANK, MLA_Q_LORA, MLA_KV_LORA, MLA_ROPE]).tolist()
    w_in = w["w_in"][i]
    (z, xbc, dt, q, k, v, r, glr, cq, ckv, kr) = [w_in[:, splits[j]:splits[j + 1]] for j in range(11)]

    def zpad(n):
        return jnp.zeros((D_MODEL, n), f32)

    half = MLA_ROPE // 2
    w_all = jnp.concatenate(
        [z, xbc, dt, zpad(P_SSD - SSD_WIDTH - SSD_CONV_CH - SSD_HEADS),
         q, k, v, r, glr, zpad(P_GLA - 2 * GLA_HEADS * GLA_DK - 2 * GLA_WIDTH - GLA_GATE_RANK),
         cq, zpad(256 - MLA_Q_LORA), ckv, kr[:, :half], zpad(128 - half), kr[:, half:], zpad(128 - half)],
        axis=1).astype(bf16)

    wuq = w["mla_w_uq"][i].reshape(MLA_Q_LORA, MLA_HEADS, MLA_NOPE + MLA_ROPE)
    nope = wuq[:, :, :MLA_NOPE].reshape(MLA_Q_LORA, MLA_HEADS * MLA_NOPE)
    ra = wuq[:, :, MLA_NOPE:MLA_NOPE + half].reshape(MLA_Q_LORA, MLA_HEADS * half)
    rb = wuq[:, :, MLA_NOPE + half:].reshape(MLA_Q_LORA, MLA_HEADS * half)
    zq = jnp.zeros((MLA_Q_LORA, 128 - MLA_HEADS * half), f32)
    wuq_x = jnp.concatenate([nope, ra, zq, rb, zq], axis=1)
    wuq_x = jnp.concatenate([wuq_x, jnp.zeros((256 - MLA_Q_LORA, 512), f32)], axis=0).astype(bf16)

    wuk = w["mla_w_uk"][i]
    wcat = jnp.zeros((MLA_HEADS, 512, QK_W), f32)
    eye = jnp.eye(half, dtype=f32)
    for h in range(MLA_HEADS):
        wcat = wcat.at[h, h * MLA_NOPE:(h + 1) * MLA_NOPE, 0:MLA_KV_LORA].set(wuk[:, h, :].T)
        wcat = wcat.at[h, 256 + h * half:256 + (h + 1) * half, MLA_KV_LORA:MLA_KV_LORA + half].set(eye)
        wcat = wcat.at[h, 384 + h * half:384 + (h + 1) * half, MLA_KV_LORA + half:MLA_KV_LORA + 2 * half].set(eye)
    sel = jnp.zeros((256, 128), f32)
    sel = sel.at[0:half, 0:half].set(eye).at[128:128 + half, half:2 * half].set(eye)
    wuv = w["mla_w_uv"][i]
    wuv_x = jnp.zeros((MLA_HEADS, MLA_KV_LORA, MLA_HEADS * MLA_DV), f32)
    for h in range(MLA_HEADS):
        wuv_x = wuv_x.at[h, :, h * MLA_DV:(h + 1) * MLA_DV].set(wuv[:, h, :])

    return {
        "w_in": w_all,
        "ssd_conv_w": w["ssd_conv_w"][i], "ssd_conv_b": row(w["ssd_conv_b"][i]),
        "ssd_dt_bias": row(w["ssd_dt_bias"][i]), "ssd_a_log": row(w["ssd_a_log"][i]),
        "ssd_d_x": row(jnp.repeat(w["ssd_d"][i], HEAD_DIM)), "ssd_norm_g": row(w["ssd_norm_g"][i]),
        "gla_gate_w": w["gla_gate_w"][i], "gla_gate_b": row(w["gla_gate_b"][i]),
        "gla_norm_g_x": row(jnp.tile(w["gla_norm_g"][i], GLA_HEADS)),
        "mla_q_norm_g_x": row(jnp.concatenate([w["mla_q_norm_g"][i], jnp.zeros((256 - MLA_Q_LORA,), f32)])),
        "mla_kv_norm_g": row(w["mla_kv_norm_g"][i]),
        "mla_w_uq_x": wuq_x, "mla_wcat": wcat.astype(bf16), "mla_sel": sel, "mla_wuv_x": wuv_x.astype(bf16),
        "w_out": w["w_out"][i].astype(bf16), "ln1_g": row(w["ln1_g"][i]), "ln1_b": row(w["ln1_b"][i]),
        "ffn_w_up": w["ffn_w_up"][i].astype(bf16), "ffn_conv_w": w["ffn_conv_w"][i],
        "ffn_conv_b": row(w["ffn_conv_b"][i]), "ffn_w_down": w["ffn_w_down"][i].astype(bf16),
        "ple_w_gate": w["ple_w_gate"][i].astype(bf16), "ple_w_proj": w["ple_w_proj"][i].astype(bf16),
        "ln2_g": row(w["ln2_g"][i]), "ln2_b": row(w["ln2_b"][i]),
    }


def _rope_tables(pos, rows):
    half = MLA_ROPE // 2
    inv = ROPE_THETA ** (-jnp.arange(half, dtype=f32) / half)
    ang = pos.astype(f32)[:, None] * inv
    pad = jnp.zeros((pos.shape[0], 128 - MLA_HEADS * half), f32)
    cos = jnp.concatenate([jnp.tile(jnp.cos(ang), (1, MLA_HEADS)), pad], axis=1)
    sin = jnp.concatenate([jnp.tile(jnp.sin(ang), (1, MLA_HEADS)), pad], axis=1)
    reps = max(1, rows // pos.shape[0])
    return jnp.tile(cos, (reps, 1)), jnp.tile(sin, (reps, 1))


def _halo(buf):
    nseq, wdt, ch = buf.shape
    return jnp.concatenate([jnp.zeros((nseq, 8 - wdt, ch), f32), buf], axis=1)


def _gla_state_in(s0):
    nseq = s0.shape[0]
    st = jnp.zeros((nseq, GLA_HEADS, GLA_DV, GLA_HEADS, GLA_DK), f32)
    for h in range(GLA_HEADS):
        st = st.at[:, h, :, h, :].set(jnp.swapaxes(s0[:, h], 1, 2))
    return st.reshape(nseq, GLA_HEADS * GLA_DV, GLA_HEADS * GLA_DK)


def _gla_state_out(st):
    nseq = st.shape[0]
    st5 = st.reshape(nseq, GLA_HEADS, GLA_DV, GLA_HEADS, GLA_DK)
    return jnp.stack([jnp.swapaxes(st5[:, h, :, h, :], 1, 2) for h in range(GLA_HEADS)], axis=1)


def _layer(x3, pe3, ssm0, conv0, gla0, ffn0, lw, cfg, cache=None):
    nseq, lp, _ = x3.shape
    c, valid, ts, nsb = cfg["c"], cfg["valid"], cfg["ts"], cfg["nsb"]
    rows = nseq * lp
    p_ssd, p_gla, p_mla = _in_proj(x3.reshape(rows, D_MODEL), lw["w_in"], min(rows, 512))
    y_ssd, ssm_h, conv_halo = _ssd(p_ssd.reshape(nseq, lp, P_SSD), _halo(conv0), ssm0, lw, c, valid)
    y_gla, gla_st = _gla(p_gla.reshape(nseq, lp, P_GLA), _gla_state_in(gla0), lw, c, valid)
    qcat, kcat, c_kv, k_rope = _mla_prep(p_mla.reshape(nseq, lp, P_MLA), cfg["cos"], cfg["sin"], lw, nsb, ts)
    if cache is None:
        y_mla = _mla_flash(qcat, kcat, lw, ts)
    else:
        cache_kv, cache_kr, page_table, layer = cache
        y_mla = _mla_paged(qcat, kcat, cache_kv, cache_kr, page_table, layer, lw, valid)
    x_out, ffn_halo = _ffn(x3, y_ssd, y_gla, y_mla, pe3, _halo(ffn0), lw, nsb, ts, valid)
    states = (c_kv[:, :valid], k_rope[:, :valid], ssm_h, conv_halo[:, 8 - (SSD_CONV - 1):],
              _gla_state_out(gla_st), ffn_halo[:, 8 - (FFN_CONV - 1):])
    return x_out, states


def _pad_rows(a, lp):
    nseq, l = a.shape[:2]
    if l == lp:
        return a
    return jnp.concatenate([a, jnp.zeros((nseq, lp - l) + a.shape[2:], a.dtype)], axis=1)


def kernel(x_prompt, x_sample, cache_kv_latent, cache_k_rope, state_ssm, state_ssm_conv, state_gla, state_ffn_conv, page_table, p_prompt, p_sample, ln_in_g, ln_in_b, w_in, ssd_conv_w, ssd_conv_b, ssd_dt_bias, ssd_a_log, ssd_d, ssd_norm_g, gla_gate_w, gla_gate_b, gla_norm_g, mla_q_norm_g, mla_w_uq, mla_kv_norm_g, mla_w_uk, mla_w_uv, w_out, ln1_g, ln1_b, ffn_w_up, ffn_conv_w, ffn_conv_b, ffn_w_down, ple_w_proj, ple_w_gate, ln2_g, ln2_b):
    weights = dict(w_in=w_in, ssd_conv_w=ssd_conv_w, ssd_conv_b=ssd_conv_b, ssd_dt_bias=ssd_dt_bias,
                   ssd_a_log=ssd_a_log, ssd_d=ssd_d, ssd_norm_g=ssd_norm_g, gla_gate_w=gla_gate_w,
                   gla_gate_b=gla_gate_b, gla_norm_g=gla_norm_g, mla_q_norm_g=mla_q_norm_g, mla_w_uq=mla_w_uq,
                   mla_kv_norm_g=mla_kv_norm_g, mla_w_uk=mla_w_uk, mla_w_uv=mla_w_uv, w_out=w_out, ln1_g=ln1_g,
                   ln1_b=ln1_b, ffn_w_up=ffn_w_up, ffn_conv_w=ffn_conv_w, ffn_conv_b=ffn_conv_b,
                   ffn_w_down=ffn_w_down, ple_w_proj=ple_w_proj, ple_w_gate=ple_w_gate, ln2_g=ln2_g, ln2_b=ln2_b)
    depth = w_in.shape[0]
    bp, lp_, _ = x_prompt.shape
    bs, ls, _ = x_sample.shape
    t_past = page_table.shape[1] * PAGE_SIZE

    g_in = ln_in_g.reshape(1, -1)
    b_in = ln_in_b.reshape(1, -1)
    hp = _ln_in(x_prompt.reshape(bp * lp_, D_MODEL), g_in, b_in, min(bp * lp_, 1024)).reshape(bp, lp_, D_MODEL)
    xs_pad = _pad_rows(x_sample, SEG_PAD)
    hs = _ln_in(xs_pad.reshape(bs * SEG_PAD, D_MODEL), g_in, b_in, bs * SEG_PAD).reshape(bs, SEG_PAD, D_MODEL)

    ts_p = min(lp_, 512)
    cos_p, sin_p = _rope_tables(jnp.arange(lp_), ts_p)
    cfg_p = dict(c=min(lp_, 128), valid=lp_, ts=ts_p, nsb=1, cos=cos_p, sin=sin_p)
    cos_s, sin_s = _rope_tables(t_past + jnp.arange(SEG_PAD), bs * SEG_PAD)
    cfg_s = dict(c=SEG_PAD, valid=ls, ts=SEG_PAD, nsb=bs, cos=cos_s, sin=sin_s)

    ssm0 = jnp.zeros((bp, SSD_HEADS, HEAD_DIM, SSD_STATE), f32)
    conv0 = jnp.zeros((bp, SSD_CONV - 1, SSD_CONV_CH), f32)
    gla0 = jnp.zeros((bp, GLA_HEADS, GLA_DK, GLA_DV), f32)
    ffn0 = jnp.zeros((bp, FFN_CONV - 1, D_FF), f32)

    prompt_states, sample_states = [], []
    for i in range(depth):
        lw = _prep_layer_weights(i, weights)
        hp, sp = _layer(hp, p_prompt[i], ssm0, conv0, gla0, ffn0, lw, cfg_p)
        hs, ss = _layer(hs, _pad_rows(p_sample[i], SEG_PAD), state_ssm[i], state_ssm_conv[i], state_gla[i],
                        state_ffn_conv[i], lw, cfg_s, cache=(cache_kv_latent, cache_k_rope, page_table, i))
        prompt_states.append(sp)
        sample_states.append(ss)
    kv_p, kr_p, ssm_p, ssm_conv_p, gla_p, ffn_conv_p = (jnp.stack(f) for f in zip(*prompt_states))
    kv_s, kr_s, ssm_s, ssm_conv_s, gla_s, ffn_conv_s = (jnp.stack(f) for f in zip(*sample_states))
    return (hp, hs[:, :ls], kv_p, kv_s, kr_p, kr_s, ssm_p, ssm_s, ssm_conv_p, ssm_conv_s, gla_p, gla_s,
            ffn_conv_p, ffn_conv_s)
```

```python
import functools
import math

import jax
import jax.numpy as jnp
import numpy as np
from jax import lax
from jax.experimental import pallas as pl
from jax.experimental.pallas import tpu as pltpu

f32 = jnp.float32
bf16 = jnp.bfloat16

D_MODEL = 1024
HEAD_DIM = 64
SSD_WIDTH = 512
SSD_HEADS = 8
SSD_GROUPS = 2
SSD_STATE = 128
SSD_CONV = 4
SSD_CONV_CH = SSD_WIDTH + 2 * SSD_GROUPS * SSD_STATE
GLA_WIDTH = 256
GLA_HEADS = 4
GLA_DK = 32
GLA_DV = 64
GLA_GATE_RANK = 16
GLA_TAU = 16.0
MLA_HEADS = 4
MLA_NOPE = 64
MLA_ROPE = 32
MLA_DV = 64
MLA_Q_LORA = 192
MLA_KV_LORA = 128
MLA_SCALE = (MLA_NOPE + MLA_ROPE) ** -0.5
EXP2_SCALE = MLA_SCALE * math.log2(math.e)
ROPE_THETA = 10000.0
D_FF = 2816
FFN_CONV = 3
PLE_DIM = 256
PAGE_SIZE = 128
DEPTH = 4
DN_ALPHA = (2 * DEPTH) ** 0.25

P_SSD = 1664
P_GLA = 896
P_MLA = 640
QK_W = 256

SEG_PAD = 8
FF_CHUNK = 256
PAGED_SLOTS = 3
PAGED_CHUNK = 2048
VMEM_LIMIT = 56 * 1024 * 1024

_HI = lax.Precision.HIGHEST


def _bdot(a, b):
    return jnp.dot(a.astype(bf16), b.astype(bf16), preferred_element_type=f32)


def _bdot_nt(a, b):
    return lax.dot_general(a.astype(bf16), b.astype(bf16), (((1,), (1,)), ((), ())),
                           preferred_element_type=f32)


def _bdot_tn(a, b):
    return lax.dot_general(a.astype(bf16), b.astype(bf16), (((0,), (0,)), ((), ())),
                           preferred_element_type=f32)


def _hdot(a, b):
    return jnp.dot(a, b, precision=_HI, preferred_element_type=f32)


def _silu(x):
    return x * jax.nn.sigmoid(x)


def _softplus(x):
    return jnp.maximum(x, 0.0) + jnp.log1p(jnp.exp(-jnp.abs(x)))


def _log_sigmoid(x):
    return jnp.minimum(x, 0.0) - jnp.log1p(jnp.exp(-jnp.abs(x)))


def _layernorm(x, g, b, eps=1e-5):
    xc = x - jnp.mean(x, -1, keepdims=True)
    var = jnp.mean(xc * xc, -1, keepdims=True)
    return xc * lax.rsqrt(var + eps) * g + b


def _tri(c):
    r = lax.broadcasted_iota(jnp.int32, (c, c), 0)
    col = lax.broadcasted_iota(jnp.int32, (c, c), 1)
    return col <= r


def _params(sem):
    return pltpu.CompilerParams(dimension_semantics=sem, vmem_limit_bytes=VMEM_LIMIT)


def _const_spec(shape):
    nd = len(shape)
    return pl.BlockSpec(shape, lambda *_: (0,) * nd)


def _ln_kernel(x_ref, g_ref, b_ref, o_ref):
    o_ref[...] = _layernorm(x_ref[...], g_ref[...], b_ref[...])


def _ln_in(x2, g, b, tm):
    rows = x2.shape[0]
    return pl.pallas_call(
        _ln_kernel,
        grid=(rows // tm,),
        in_specs=[pl.BlockSpec((tm, D_MODEL), lambda i: (i, 0)), _const_spec((1, D_MODEL)), _const_spec((1, D_MODEL))],
        out_specs=pl.BlockSpec((tm, D_MODEL), lambda i: (i, 0)),
        out_shape=jax.ShapeDtypeStruct((rows, D_MODEL), f32),
        compiler_params=_params(("parallel",)),
        name="ln_in",
    )(x2, g, b)


def _in_proj_kernel(x_ref, w_ref, ssd_ref, gla_ref, mla_ref):
    x = x_ref[...].astype(bf16)
    ssd_ref[...] = jnp.dot(x, w_ref[:, 0:P_SSD], preferred_element_type=f32)
    gla_ref[...] = jnp.dot(x, w_ref[:, P_SSD:P_SSD + P_GLA], preferred_element_type=f32)
    mla_ref[...] = jnp.dot(x, w_ref[:, P_SSD + P_GLA:], preferred_element_type=f32)


def _in_proj(x2, w, tm):
    rows = x2.shape[0]
    wtot = P_SSD + P_GLA + P_MLA
    return pl.pallas_call(
        _in_proj_kernel,
        grid=(rows // tm,),
        in_specs=[pl.BlockSpec((tm, D_MODEL), lambda i: (i, 0)), _const_spec((D_MODEL, wtot))],
        out_specs=[pl.BlockSpec((tm, P_SSD), lambda i: (i, 0)),
                   pl.BlockSpec((tm, P_GLA), lambda i: (i, 0)),
                   pl.BlockSpec((tm, P_MLA), lambda i: (i, 0))],
        out_shape=[jax.ShapeDtypeStruct((rows, P_SSD), f32),
                   jax.ShapeDtypeStruct((rows, P_GLA), f32),
                   jax.ShapeDtypeStruct((rows, P_MLA), f32)],
        compiler_params=_params(("parallel",)),
        name="in_proj",
    )(x2, w)


def _ssd_kernel(c, valid, p_ref, halo0_ref, h0_ref, cw_ref, cb_ref, dtb_ref, alog_ref, dexp_ref, ng_ref,
                y_ref, hout_ref, halo_out_ref, ht_scr, halo_scr):
    ci = pl.program_id(1)
    hp = SSD_HEADS * HEAD_DIM
    gw = hp // SSD_GROUPS

    @pl.when(ci == 0)
    def _():
        ht_scr[...] = h0_ref[0].reshape(hp, SSD_STATE).T
        halo_scr[...] = halo0_ref[0]

    p = p_ref[0]
    z = p[:, 0:SSD_WIDTH]
    xr = p[:, SSD_WIDTH:SSD_WIDTH + SSD_CONV_CH]
    dt_raw = p[:, SSD_WIDTH + SSD_CONV_CH:SSD_WIDTH + SSD_CONV_CH + SSD_HEADS]

    halo = halo_scr[...]
    row = lax.broadcasted_iota(jnp.int32, (c, 1), 0)
    hm1, hm2, hm3 = halo[7:8, :], halo[6:7, :], halo[5:6, :]
    s1 = jnp.where(row >= 1, pltpu.roll(xr, 1, axis=0), hm1)
    s2 = jnp.where(row >= 2, pltpu.roll(xr, 2, axis=0), jnp.where(row == 1, hm1, hm2))
    s3 = jnp.where(row >= 3, pltpu.roll(xr, 3, axis=0),
                   jnp.where(row == 2, hm1, jnp.where(row == 1, hm2, hm3)))
    cw = cw_ref[...]
    conv = cw[3:4, :] * xr + cw[2:3, :] * s1 + cw[1:2, :] * s2 + cw[0:1, :] * s3 + cb_ref[...]
    last = xr[c - 8:c, :]
    if valid < c:
        last = pltpu.roll(last, c - valid, axis=0)
    halo_scr[...] = last

    xbc = _silu(conv)
    xs = xbc[:, 0:SSD_WIDTH]
    bs = xbc[:, SSD_WIDTH:SSD_WIDTH + SSD_GROUPS * SSD_STATE]
    cs = xbc[:, SSD_WIDTH + SSD_GROUPS * SSD_STATE:]

    dt = _softplus(dt_raw + dtb_ref[...])
    if valid < c:
        dt = jnp.where(row < valid, dt, 0.0)
    la = dt * (-jnp.exp(alog_ref[...]))
    tri = _tri(c)
    cum = _hdot(tri.astype(f32), la)
    cum_t = cum.T
    dt_t = dt.T
    ecum = jnp.exp(cum)
    w = jnp.exp(cum[c - 1:c, :] - cum) * dt
    hl = lax.broadcasted_iota(jnp.int32, (SSD_HEADS, hp), 1) // HEAD_DIM
    hh = lax.broadcasted_iota(jnp.int32, (SSD_HEADS, hp), 0)
    expand = (hl == hh).astype(f32)
    ecum_x = _hdot(ecum, expand)
    w_x = _hdot(w, expand)

    ht = ht_scr[...]
    lane_head = lax.broadcasted_iota(jnp.int32, (1, gw), 1) // HEAD_DIM
    heads_per_group = SSD_HEADS // SSD_GROUPS
    y_parts = []
    for g in range(SSD_GROUPS):
        sl = slice(g * gw, (g + 1) * gw)
        b_g = bs[:, g * SSD_STATE:(g + 1) * SSD_STATE]
        c_g = cs[:, g * SSD_STATE:(g + 1) * SSD_STATE]
        x_g = xs[:, sl]
        scores = _bdot_nt(c_g, b_g)
        y_g = _bdot(c_g, ht[:, sl]) * ecum_x[:, sl]
        for hh_ in range(heads_per_group):
            h = g * heads_per_group + hh_
            seg = cum[:, h:h + 1] - cum_t[h:h + 1, :]
            decay = jnp.exp(jnp.where(tri, seg, -jnp.inf))
            m = scores * decay * dt_t[h:h + 1, :]
            xm = jnp.where(lane_head == hh_, x_g, 0.0)
            y_g = y_g + _bdot(m, xm)
        y_parts.append(y_g)
        ht_scr[:, sl] = ht[:, sl] * ecum_x[c - 1:c, sl] + _bdot_tn(b_g, x_g * w_x[:, sl])
    y = jnp.concatenate(y_parts, axis=-1)
    y = (y + dexp_ref[...] * xs) * _silu(z)
    ng = ng_ref[...]
    outs = []
    for g in range(SSD_GROUPS):
        sl = slice(g * gw, (g + 1) * gw)
        yg = y[:, sl]
        outs.append(yg * lax.rsqrt(jnp.mean(yg * yg, -1, keepdims=True) + 1e-6) * ng[:, sl])
    y_ref[0] = jnp.concatenate(outs, axis=-1).astype(bf16)

    @pl.when(ci == pl.num_programs(1) - 1)
    def _():
        hout_ref[0] = ht_scr[...].T.reshape(SSD_HEADS, HEAD_DIM, SSD_STATE)
        halo_out_ref[0] = halo_scr[...]


def _ssd(p_ssd, halo0, h0, lw, c, valid):
    nseq, lp, _ = p_ssd.shape
    hp = SSD_HEADS * HEAD_DIM
    return pl.pallas_call(
        functools.partial(_ssd_kernel, c, valid),
        grid=(nseq, lp // c),
        in_specs=[pl.BlockSpec((1, c, P_SSD), lambda b, i: (b, i, 0)),
                  pl.BlockSpec((1, 8, SSD_CONV_CH), lambda b, i: (b, 0, 0)),
                  pl.BlockSpec((1, SSD_HEADS, HEAD_DIM, SSD_STATE), lambda b, i: (b, 0, 0, 0)),
                  _const_spec((SSD_CONV, SSD_CONV_CH)), _const_spec((1, SSD_CONV_CH)),
                  _const_spec((1, SSD_HEADS)), _const_spec((1, SSD_HEADS)),
                  _const_spec((1, hp)), _const_spec((1, hp))],
        out_specs=[pl.BlockSpec((1, c, hp), lambda b, i: (b, i, 0)),
                   pl.BlockSpec((1, SSD_HEADS, HEAD_DIM, SSD_STATE), lambda b, i: (b, 0, 0, 0)),
                   pl.BlockSpec((1, 8, SSD_CONV_CH), lambda b, i: (b, 0, 0))],
        out_shape=[jax.ShapeDtypeStruct((nseq, lp, hp), bf16),
                   jax.ShapeDtypeStruct((nseq, SSD_HEADS, HEAD_DIM, SSD_STATE), f32),
                   jax.ShapeDtypeStruct((nseq, 8, SSD_CONV_CH), f32)],
        scratch_shapes=[pltpu.VMEM((SSD_STATE, hp), f32), pltpu.VMEM((8, SSD_CONV_CH), f32)],
        compiler_params=_params(("parallel", "arbitrary")),
        name="ssd",
    )(p_ssd, halo0, h0, lw["ssd_conv_w"], lw["ssd_conv_b"], lw["ssd_dt_bias"], lw["ssd_a_log"],
      lw["ssd_d_x"], lw["ssd_norm_g"])


def _gla_kernel(c, valid, p_ref, st0_ref, gw_ref, gb_ref, gn_ref, y_ref, st_out_ref, st_scr):
    ci = pl.program_id(1)
    hk = GLA_HEADS * GLA_DK
    hv = GLA_HEADS * GLA_DV

    @pl.when(ci == 0)
    def _():
        st_scr[...] = st0_ref[0]

    p = p_ref[0]
    q = p[:, 0:hk] * GLA_DK ** -0.5
    k = p[:, hk:2 * hk]
    v = p[:, 2 * hk:2 * hk + hv]
    r = p[:, 2 * hk + hv:2 * hk + 2 * hv]
    glr = p[:, 2 * hk + 2 * hv:2 * hk + 2 * hv + GLA_GATE_RANK]
    log_a = _log_sigmoid(_hdot(glr, gw_ref[...]) + gb_ref[...]) / GLA_TAU
    if valid < c:
        row = lax.broadcasted_iota(jnp.int32, (c, 1), 0)
        log_a = jnp.where(row < valid, log_a, 0.0)
        k = jnp.where(row < valid, k, 0.0)
    tri = _tri(c)
    b = _hdot(tri.astype(f32), log_a)
    b_last = b[c - 1:c, :]
    qt = q * jnp.exp(b)
    kt = k * jnp.exp(-b)
    khat = k * jnp.exp(b_last - b)

    st = st_scr[...]
    o = _bdot_nt(qt, st)
    lane_k = lax.broadcasted_iota(jnp.int32, (1, hk), 1) // GLA_DK
    lane_v = lax.broadcasted_iota(jnp.int32, (1, hv), 1) // GLA_DV
    for h in range(GLA_HEADS):
        att = _bdot_nt(jnp.where(lane_k == h, qt, 0.0), kt)
        att = jnp.where(tri, att, 0.0)
        o = o + _bdot(att, jnp.where(lane_v == h, v, 0.0))
    blk = (lax.broadcasted_iota(jnp.int32, (hv, hk), 0) // GLA_DV
           == lax.broadcasted_iota(jnp.int32, (hv, hk), 1) // GLA_DK)
    st_scr[...] = jnp.where(blk, st * jnp.exp(b_last) + _bdot_tn(v, khat), 0.0)

    oo = o * o
    oo_hi = oo.astype(bf16)
    oo_lo = (oo - oo_hi.astype(f32)).astype(bf16)
    grp = (lax.broadcasted_iota(jnp.int32, (hv, hv), 0) // GLA_DV
           == lax.broadcasted_iota(jnp.int32, (hv, hv), 1) // GLA_DV)
    gmat = jnp.where(grp, 1.0 / GLA_DV, 0.0).astype(bf16)
    ms = jnp.dot(oo_hi, gmat, preferred_element_type=f32) + jnp.dot(oo_lo, gmat, preferred_element_type=f32)
    y = o * lax.rsqrt(ms + 1e-6) * gn_ref[...] * _silu(r)
    y_ref[0] = y.astype(bf16)

    @pl.when(ci == pl.num_programs(1) - 1)
    def _():
        st_out_ref[0] = st_scr[...]


def _gla(p_gla, st0, lw, c, valid):
    nseq, lp, _ = p_gla.shape
    hk = GLA_HEADS * GLA_DK
    hv = GLA_HEADS * GLA_DV
    return pl.pallas_call(
        functools.partial(_gla_kernel, c, valid),
        grid=(nseq, lp // c),
        in_specs=[pl.BlockSpec((1, c, P_GLA), lambda b, i: (b, i, 0)),
                  pl.BlockSpec((1, hv, hk), lambda b, i: (b, 0, 0)),
                  _const_spec((GLA_GATE_RANK, hk)), _const_spec((1, hk)), _const_spec((1, hv))],
        out_specs=[pl.BlockSpec((1, c, hv), lambda b, i: (b, i, 0)),
                   pl.BlockSpec((1, hv, hk), lambda b, i: (b, 0, 0))],
        out_shape=[jax.ShapeDtypeStruct((nseq, lp, hv), bf16),
                   jax.ShapeDtypeStruct((nseq, hv, hk), f32)],
        scratch_shapes=[pltpu.VMEM((hv, hk), f32)],
        compiler_params=_params(("parallel", "arbitrary")),
        name="gla",
    )(p_gla, st0, lw["gla_gate_w"], lw["gla_gate_b"], lw["gla_norm_g_x"])


def _mla_prep_kernel(nsb, ts, p_ref, cos_ref, sin_ref, qg_ref, kvg_ref, wuq_ref, wcat_ref, sel_ref,
                     qcat_ref, kcat_ref, ckv_ref, krope_ref):
    rows = nsb * ts
    p = p_ref[...].reshape(rows, P_MLA)
    cq = p[:, 0:256]
    ckv = p[:, 256:384]
    kra = p[:, 384:512]
    krb = p[:, 512:640]
    cos = cos_ref[...]
    sin = sin_ref[...]
    cqn = cq * lax.rsqrt(jnp.sum(cq * cq, -1, keepdims=True) * (1.0 / MLA_Q_LORA) + 1e-6) * qg_ref[...]
    qf = _bdot(cqn, wuq_ref[...])
    ra = qf[:, 256:384]
    rb = qf[:, 384:512]
    feat = jnp.concatenate([qf[:, 0:256], ra * cos - rb * sin, ra * sin + rb * cos], axis=-1).astype(bf16)
    for h in range(MLA_HEADS):
        qh = jnp.dot(feat, wcat_ref[h], preferred_element_type=f32).astype(bf16)
        qcat_ref[:, h] = qh.reshape(nsb, ts, QK_W)
    c_kv = ckv * lax.rsqrt(jnp.mean(ckv * ckv, -1, keepdims=True) + 1e-6) * kvg_ref[...]
    kr2 = jnp.concatenate([kra * cos - krb * sin, kra * sin + krb * cos], axis=-1)
    kr128 = _hdot(kr2, sel_ref[...])
    ckv_ref[...] = c_kv.reshape(nsb, ts, MLA_KV_LORA)
    krope_ref[...] = kr128[:, 0:MLA_ROPE].reshape(nsb, ts, MLA_ROPE)
    one_lane = (lax.broadcasted_iota(jnp.int32, (1, 128), 1) == 127).astype(f32)
    kcat_ref[...] = jnp.concatenate([c_kv, kr128 + one_lane], axis=-1).astype(bf16).reshape(nsb, ts, QK_W)


def _mla_prep(p_mla, cos, sin, lw, nsb, ts):
    nseq, lp, _ = p_mla.shape
    rows = nsb * ts
    ntile = lp // ts
    tbl_tiles = cos.shape[0] // rows
    return pl.pallas_call(
        functools.partial(_mla_prep_kernel, nsb, ts),
        grid=(nseq // nsb, ntile),
        in_specs=[pl.BlockSpec((nsb, ts, P_MLA), lambda b, i: (b, i, 0)),
                  pl.BlockSpec((rows, 128), lambda b, i: (i % tbl_tiles, 0)),
                  pl.BlockSpec((rows, 128), lambda b, i: (i % tbl_tiles, 0)),
                  _const_spec((1, 256)), _const_spec((1, MLA_KV_LORA)),
                  _const_spec((256, 512)), _const_spec((MLA_HEADS, 512, QK_W)), _const_spec((256, 128))],
        out_specs=[pl.BlockSpec((nsb, MLA_HEADS, ts, QK_W), lambda b, i: (b, 0, i, 0)),
                   pl.BlockSpec((nsb, ts, QK_W), lambda b, i: (b, i, 0)),
                   pl.BlockSpec((nsb, ts, MLA_KV_LORA), lambda b, i: (b, i, 0)),
                   pl.BlockSpec((nsb, ts, MLA_ROPE), lambda b, i: (b, i, 0))],
        out_shape=[jax.ShapeDtypeStruct((nseq, MLA_HEADS, lp, QK_W), bf16),
                   jax.ShapeDtypeStruct((nseq, lp, QK_W), bf16),
                   jax.ShapeDtypeStruct((nseq, lp, MLA_KV_LORA), f32),
                   jax.ShapeDtypeStruct((nseq, lp, MLA_ROPE), f32)],
        compiler_params=_params(("parallel", "parallel")),
        name="mla_prep",
    )(p_mla, cos, sin, lw["mla_q_norm_g_x"], lw["mla_kv_norm_g"], lw["mla_w_uq_x"], lw["mla_wcat"],
      lw["mla_sel"])


def _mla_flash_kernel(tq, tk, q_ref, k_ref, wuv_ref, y_ref, m_scr, acc_scr):
    qi = pl.program_id(1)
    ki = pl.program_id(2)

    @pl.when(ki == 0)
    def _():
        m_scr[...] = jnp.full(m_scr.shape, -jnp.inf, f32)
        acc_scr[...] = jnp.zeros(acc_scr.shape, f32)

    def step(masked):
        k = k_ref[0]
        for h in range(MLA_HEADS):
            s = lax.dot_general(q_ref[0, h], k, (((1,), (1,)), ((), ())), preferred_element_type=f32)
            if masked:
                rpos = lax.broadcasted_iota(jnp.int32, (tq, tk), 0)
                cpos = lax.broadcasted_iota(jnp.int32, (tq, tk), 1)
                s = jnp.where(cpos <= rpos, s, -jnp.inf)
            m_prev = m_scr[h]
            m_new = jnp.maximum(m_prev, jnp.max(s, -1, keepdims=True))
            alpha = jnp.exp2((m_prev - m_new) * EXP2_SCALE)
            pr = jnp.exp2((s - m_new) * EXP2_SCALE)
            acc_scr[h] = alpha * acc_scr[h] + jnp.dot(pr.astype(bf16), k, preferred_element_type=f32)
            m_scr[h] = m_new

    @pl.when(ki < qi)
    def _():
        step(False)

    @pl.when(ki == qi)
    def _():
        step(True)
        y = jnp.zeros((tq, MLA_HEADS * MLA_DV), f32)
        for h in range(MLA_HEADS):
            acc = acc_scr[h]
            y = y + _bdot(acc[:, 0:MLA_KV_LORA] / acc[:, QK_W - 1:QK_W], wuv_ref[h])
        y_ref[0] = y.astype(bf16)


def _mla_flash(qcat, kcat, lw, tq):
    nseq, _, lp, _ = qcat.shape
    tk = tq
    nq = lp // tq
    return pl.pallas_call(
        functools.partial(_mla_flash_kernel, tq, tk),
        grid=(nseq, nq, nq),
        in_specs=[pl.BlockSpec((1, MLA_HEADS, tq, QK_W), lambda b, i, j: (b, 0, i, 0)),
                  pl.BlockSpec((1, tk, QK_W), lambda b, i, j: (b, jnp.minimum(i, j), 0)),
                  _const_spec((MLA_HEADS, MLA_KV_LORA, MLA_HEADS * MLA_DV))],
        out_specs=pl.BlockSpec((1, tq, MLA_HEADS * MLA_DV), lambda b, i, j: (b, i, 0)),
        out_shape=jax.ShapeDtypeStruct((nseq, lp, MLA_HEADS * MLA_DV), bf16),
        scratch_shapes=[pltpu.VMEM((MLA_HEADS, tq, 1), f32), pltpu.VMEM((MLA_HEADS, tq, QK_W), f32)],
        compiler_params=_params(("parallel", "parallel", "arbitrary")),
        name="mla_flash",
    )(qcat, kcat, lw["mla_wuv_x"])


def _mla_paged_kernel(valid, n_pages, layer, pt_ref, q_ref, knew_ref, wuv_ref, kv_hbm, krt_hbm, y_ref,
                      kv_buf, krt_buf, kvb_scr, s_scr, sem):
    b = pl.program_id(0)
    nb = pl.num_programs(0)
    m_rows = MLA_HEADS * SEG_PAD
    t_past = n_pages * PAGE_SIZE
    chunk = math.gcd(t_past, PAGED_CHUNK)

    def page_copies(seq, i, slot):
        pg = pt_ref[seq, i]
        rows = pl.ds(i * PAGE_SIZE, PAGE_SIZE)
        return (pltpu.make_async_copy(kv_hbm.at[layer, pg], kv_buf.at[slot, rows, :], sem.at[0, slot]),
                pltpu.make_async_copy(krt_hbm.at[layer, pg], krt_buf.at[slot, :, rows], sem.at[1, slot]))

    def start_seq(seq, slot):
        for i in range(n_pages):
            for cp in page_copies(seq, i, slot):
                cp.start()

    def wait_seq(seq, slot):
        for i in range(n_pages):
            for cp in page_copies(seq, i, slot):
                cp.wait()

    @pl.when(b == 0)
    def _():
        start_seq(0, 0)
        start_seq(jnp.minimum(1, nb - 1), 1)

    slot = b % PAGED_SLOTS
    wait_seq(b, slot)
    start_seq(jnp.minimum(b + 2, nb - 1), (b + 2) % PAGED_SLOTS)

    q = q_ref[0].reshape(m_rows, QK_W)
    q_lat = q[:, 0:MLA_KV_LORA]
    q_rope = q[:, MLA_KV_LORA:MLA_KV_LORA + MLA_ROPE]
    for ci in range(t_past // chunk):
        cols = pl.ds(ci * chunk, chunk)
        kvc = kv_buf[slot, cols, :].astype(bf16)
        kvb_scr[cols, :] = kvc
        s_scr[:, cols] = (lax.dot_general(q_lat, kvc, (((1,), (1,)), ((), ())), preferred_element_type=f32)
                          + jnp.dot(q_rope, krt_buf[slot, :, cols].astype(bf16), preferred_element_type=f32))

    knew = knew_ref[0]
    sn = lax.dot_general(q, knew, (((1,), (1,)), ((), ())), preferred_element_type=f32)
    rpos = lax.broadcasted_iota(jnp.int32, (MLA_HEADS, SEG_PAD, SEG_PAD), 1).reshape(m_rows, SEG_PAD)
    cpos = lax.broadcasted_iota(jnp.int32, (m_rows, SEG_PAD), 1)
    sn = jnp.where((cpos <= rpos) & (cpos < valid), sn, -jnp.inf)
    m = jnp.maximum(jnp.max(s_scr[...], -1, keepdims=True), jnp.max(sn, -1, keepdims=True))
    pn = jnp.exp((sn - m) * MLA_SCALE)
    l = jnp.sum(pn, -1, keepdims=True)
    acc = jnp.dot(pn.astype(bf16), knew[:, 0:MLA_KV_LORA], preferred_element_type=f32)
    for ci in range(t_past // chunk):
        cols = pl.ds(ci * chunk, chunk)
        pr = jnp.exp((s_scr[:, cols] - m) * MLA_SCALE)
        l = l + jnp.sum(pr, -1, keepdims=True)
        acc = acc + jnp.dot(pr.astype(bf16), kvb_scr[cols, :], preferred_element_type=f32)
    o_lat = acc / l
    y = jnp.zeros((SEG_PAD, MLA_HEADS * MLA_DV), f32)
    for h in range(MLA_HEADS):
        y = y + _bdot(o_lat[h * SEG_PAD:(h + 1) * SEG_PAD, :], wuv_ref[h])
    y_ref[0] = y.astype(bf16)

    @pl.when(b == nb - 1)
    def _():
        wait_seq(b, (b + 1) % PAGED_SLOTS)
        wait_seq(b, (b + 2) % PAGED_SLOTS)


def _mla_paged(qcat, kcat, cache_kv, cache_krt, page_table, layer, lw, valid):
    nseq = qcat.shape[0]
    n_pages = page_table.shape[1]
    t_past = n_pages * PAGE_SIZE
    grid_spec = pltpu.PrefetchScalarGridSpec(
        num_scalar_prefetch=1,
        grid=(nseq,),
        in_specs=[pl.BlockSpec((1, MLA_HEADS, SEG_PAD, QK_W), lambda b, pt: (b, 0, 0, 0)),
                  pl.BlockSpec((1, SEG_PAD, QK_W), lambda b, pt: (b, 0, 0)),
                  pl.BlockSpec((MLA_HEADS, MLA_KV_LORA, MLA_HEADS * MLA_DV), lambda b, pt: (0, 0, 0)),
                  pl.BlockSpec(memory_space=pl.ANY), pl.BlockSpec(memory_space=pl.ANY)],
        out_specs=pl.BlockSpec((1, SEG_PAD, MLA_HEADS * MLA_DV), lambda b, pt: (b, 0, 0)),
        scratch_shapes=[pltpu.VMEM((PAGED_SLOTS, t_past, MLA_KV_LORA), f32),
                        pltpu.VMEM((PAGED_SLOTS, MLA_ROPE, t_past), f32),
                        pltpu.VMEM((t_past, MLA_KV_LORA), bf16),
                        pltpu.VMEM((MLA_HEADS * SEG_PAD, t_past), f32),
                        pltpu.SemaphoreType.DMA((2, PAGED_SLOTS))],
    )
    return pl.pallas_call(
        functools.partial(_mla_paged_kernel, valid, n_pages, layer),
        grid_spec=grid_spec,
        out_shape=jax.ShapeDtypeStruct((nseq, SEG_PAD, MLA_HEADS * MLA_DV), bf16),
        compiler_params=_params(("arbitrary",)),
        name="mla_paged",
    )(page_table, qcat, kcat, lw["mla_wuv_x"], cache_kv, cache_krt)


def _ffn_kernel(nsb, ts, valid, x_ref, ys_ref, yg_ref, ym_ref, pe_ref, halo0_ref,
                wout_ref, g1_ref, b1_ref, wup_ref, fcw_ref, fcb_ref, wdn_ref, wpg_ref, wpp_ref, g2_ref, b2_ref,
                o_ref, halo_out_ref, halo_scr, hmid_scr):
    ti = pl.program_id(1)
    rows = nsb * ts

    @pl.when(ti == 0)
    def _():
        halo_scr[...] = halo0_ref[...]

    x = x_ref[...].reshape(rows, D_MODEL)
    ycat = jnp.concatenate([ys_ref[...].reshape(rows, SSD_WIDTH), yg_ref[...].reshape(rows, GLA_WIDTH),
                            ym_ref[...].reshape(rows, MLA_HEADS * MLA_DV)], axis=-1)
    mix = jnp.dot(ycat, wout_ref[...], preferred_element_type=f32)
    x1 = _layernorm(DN_ALPHA * x + mix, g1_ref[...], b1_ref[...])
    x1b = x1.astype(bf16)

    pos = lax.broadcasted_iota(jnp.int32, (nsb, ts, 1), 1)
    for ch in range(D_FF // FF_CHUNK):
        lo = ch * FF_CHUNK
        a = jnp.dot(x1b, wup_ref[:, lo:lo + FF_CHUNK], preferred_element_type=f32)
        bgate = jnp.dot(x1b, wup_ref[:, D_FF + lo:D_FF + lo + FF_CHUNK], preferred_element_type=f32)
        a3 = a.reshape(nsb, ts, FF_CHUNK)
        halo = halo_scr[:, :, lo:lo + FF_CHUNK]
        hm1, hm2 = halo[:, 7:8, :], halo[:, 6:7, :]
        s1 = jnp.where(pos >= 1, pltpu.roll(a3, 1, axis=1), hm1)
        s2 = jnp.where(pos >= 2, pltpu.roll(a3, 2, axis=1), jnp.where(pos == 1, hm1, hm2))
        fcw = fcw_ref[:, lo:lo + FF_CHUNK]
        ac = fcw[2:3, :] * a3 + fcw[1:2, :] * s1 + fcw[0:1, :] * s2 + fcb_ref[:, lo:lo + FF_CHUNK]
        last = a3[:, ts - 8:ts, :]
        if valid < ts:
            last = pltpu.roll(last, ts - valid, axis=1)
        halo_scr[:, :, lo:lo + FF_CHUNK] = last
        ge = 0.5 * ac * (1.0 + lax.erf(ac * (1.0 / math.sqrt(2.0))))
        hmid_scr[:, lo:lo + FF_CHUNK] = (ge.reshape(rows, FF_CHUNK) * bgate).astype(bf16)
    f = jnp.dot(hmid_scr[...], wdn_ref[...], preferred_element_type=f32)

    gate = jax.nn.sigmoid(jnp.dot(x1b, wpg_ref[...], preferred_element_type=f32))
    pe = _bdot(pe_ref[...].reshape(rows, PLE_DIM), wpp_ref[...])
    x2 = _layernorm(DN_ALPHA * x1 + f + gate * pe, g2_ref[...], b2_ref[...])
    o_ref[...] = x2.reshape(nsb, ts, D_MODEL)

    @pl.when(ti == pl.num_programs(1) - 1)
    def _():
        halo_out_ref[...] = halo_scr[...]


def _ffn(x3, ys, yg, ym, pe4, layer, halo0, lw, nsb, ts, valid):
    nseq, lp, _ = x3.shape
    pe_spec = pl.BlockSpec((None, nsb, ts, PLE_DIM), lambda b, i: (layer, b, i, 0))

    def tok(width):
        return pl.BlockSpec((nsb, ts, width), lambda b, i: (b, i, 0))

    def seq(width):
        return pl.BlockSpec((nsb, 8, width), lambda b, i: (b, 0, 0))

    return pl.pallas_call(
        functools.partial(_ffn_kernel, nsb, ts, valid),
        grid=(nseq // nsb, lp // ts),
        in_specs=[tok(D_MODEL), tok(SSD_WIDTH), tok(GLA_WIDTH), tok(MLA_HEADS * MLA_DV), pe_spec, seq(D_FF),
                  _const_spec((D_MODEL, D_MODEL)), _const_spec((1, D_MODEL)), _const_spec((1, D_MODEL)),
                  _const_spec((D_MODEL, 2 * D_FF)), _const_spec((FFN_CONV, D_FF)), _const_spec((1, D_FF)),
                  _const_spec((D_FF, D_MODEL)), _const_spec((D_MODEL, D_MODEL)), _const_spec((PLE_DIM, D_MODEL)),
                  _const_spec((1, D_MODEL)), _const_spec((1, D_MODEL))],
        out_specs=[tok(D_MODEL), seq(D_FF)],
        out_shape=[jax.ShapeDtypeStruct((nseq, lp, D_MODEL), f32),
                   jax.ShapeDtypeStruct((nseq, 8, D_FF), f32)],
        scratch_shapes=[pltpu.VMEM((nsb, 8, D_FF), f32), pltpu.VMEM((nsb * ts, D_FF), bf16)],
        compiler_params=_params(("parallel", "arbitrary")),
        name="out_ffn",
    )(x3, ys, yg, ym, pe4, halo0, lw["w_out"], lw["ln1_g"], lw["ln1_b"], lw["ffn_w_up"], lw["ffn_conv_w"],
      lw["ffn_conv_b"], lw["ffn_w_down"], lw["ple_w_gate"], lw["ple_w_proj"], lw["ln2_g"], lw["ln2_b"])


def _prep_layer_weights(i, w):
    def row(v):
        return v.reshape(1, -1).astype(f32)

    splits = np.cumsum([0, SSD_WIDTH, SSD_CONV_CH, SSD_HEADS, GLA_HEADS * GLA_DK, GLA_HEADS * GLA_DK, GLA_WIDTH,
                        GLA_WIDTH, GLA_GATE_RANK, MLA_Q_LORA, MLA_KV_LORA, MLA_ROPE]).tolist()
    w_in = w["w_in"][i]
    (z, xbc, dt, q, k, v, r, glr, cq, ckv, kr) = [w_in[:, splits[j]:splits[j + 1]] for j in range(11)]

    def zpad(n):
        return jnp.zeros((D_MODEL, n), f32)

    half = MLA_ROPE // 2
    w_all = jnp.concatenate(
        [z, xbc, dt, zpad(P_SSD - SSD_WIDTH - SSD_CONV_CH - SSD_HEADS),
         q, k, v, r, glr, zpad(P_GLA - 2 * GLA_HEADS * GLA_DK - 2 * GLA_WIDTH - GLA_GATE_RANK),
         cq, zpad(256 - MLA_Q_LORA), ckv, kr[:, :half], zpad(128 - half), kr[:, half:], zpad(128 - half)],
        axis=1).astype(bf16)

    wuq = w["mla_w_uq"][i].reshape(MLA_Q_LORA, MLA_HEADS, MLA_NOPE + MLA_ROPE)
    nope = wuq[:, :, :MLA_NOPE].reshape(MLA_Q_LORA, MLA_HEADS * MLA_NOPE)
    ra = wuq[:, :, MLA_NOPE:MLA_NOPE + half].reshape(MLA_Q_LORA, MLA_HEADS * half)
    rb = wuq[:, :, MLA_NOPE + half:].reshape(MLA_Q_LORA, MLA_HEADS * half)
    zq = jnp.zeros((MLA_Q_LORA, 128 - MLA_HEADS * half), f32)
    wuq_x = jnp.concatenate([nope, ra, zq, rb, zq], axis=1)
    wuq_x = jnp.concatenate([wuq_x, jnp.zeros((256 - MLA_Q_LORA, 512), f32)], axis=0).astype(bf16)

    wuk = w["mla_w_uk"][i]
    wcat = jnp.zeros((MLA_HEADS, 512, QK_W), f32)
    eye = jnp.eye(half, dtype=f32)
    for h in range(MLA_HEADS):
        wcat = wcat.at[h, h * MLA_NOPE:(h + 1) * MLA_NOPE, 0:MLA_KV_LORA].set(wuk[:, h, :].T)
        wcat = wcat.at[h, 256 + h * half:256 + (h + 1) * half, MLA_KV_LORA:MLA_KV_LORA + half].set(eye)
        wcat = wcat.at[h, 384 + h * half:384 + (h + 1) * half, MLA_KV_LORA + half:MLA_KV_LORA + 2 * half].set(eye)
    sel = jnp.zeros((256, 128), f32)
    sel = sel.at[0:half, 0:half].set(eye).at[128:128 + half, half:2 * half].set(eye)
    wuv = w["mla_w_uv"][i]
    wuv_x = jnp.zeros((MLA_HEADS, MLA_KV_LORA, MLA_HEADS * MLA_DV), f32)
    for h in range(MLA_HEADS):
        wuv_x = wuv_x.at[h, :, h * MLA_DV:(h + 1) * MLA_DV].set(wuv[:, h, :])

    return {
        "w_in": w_all,
        "ssd_conv_w": w["ssd_conv_w"][i], "ssd_conv_b": row(w["ssd_conv_b"][i]),
        "ssd_dt_bias": row(w["ssd_dt_bias"][i]), "ssd_a_log": row(w["ssd_a_log"][i]),
        "ssd_d_x": row(jnp.repeat(w["ssd_d"][i], HEAD_DIM)), "ssd_norm_g": row(w["ssd_norm_g"][i]),
        "gla_gate_w": w["gla_gate_w"][i], "gla_gate_b": row(w["gla_gate_b"][i]),
        "gla_norm_g_x": row(jnp.tile(w["gla_norm_g"][i], GLA_HEADS)),
        "mla_q_norm_g_x": row(jnp.concatenate([w["mla_q_norm_g"][i], jnp.zeros((256 - MLA_Q_LORA,), f32)])),
        "mla_kv_norm_g": row(w["mla_kv_norm_g"][i]),
        "mla_w_uq_x": wuq_x, "mla_wcat": wcat.astype(bf16), "mla_sel": sel, "mla_wuv_x": wuv_x.astype(bf16),
        "w_out": w["w_out"][i].astype(bf16), "ln1_g": row(w["ln1_g"][i]), "ln1_b": row(w["ln1_b"][i]),
        "ffn_w_up": w["ffn_w_up"][i].astype(bf16), "ffn_conv_w": w["ffn_conv_w"][i],
        "ffn_conv_b": row(w["ffn_conv_b"][i]), "ffn_w_down": w["ffn_w_down"][i].astype(bf16),
        "ple_w_gate": w["ple_w_gate"][i].astype(bf16), "ple_w_proj": w["ple_w_proj"][i].astype(bf16),
        "ln2_g": row(w["ln2_g"][i]), "ln2_b": row(w["ln2_b"][i]),
    }


def _rope_tables(pos, rows):
    half = MLA_ROPE // 2
    inv = ROPE_THETA ** (-jnp.arange(half, dtype=f32) / half)
    ang = pos.astype(f32)[:, None] * inv
    pad = jnp.zeros((pos.shape[0], 128 - MLA_HEADS * half), f32)
    cos = jnp.concatenate([jnp.tile(jnp.cos(ang), (1, MLA_HEADS)), pad], axis=1)
    sin = jnp.concatenate([jnp.tile(jnp.sin(ang), (1, MLA_HEADS)), pad], axis=1)
    reps = max(1, rows // pos.shape[0])
    return jnp.tile(cos, (reps, 1)), jnp.tile(sin, (reps, 1))


def _halo(buf):
    nseq, wdt, ch = buf.shape
    return jnp.concatenate([jnp.zeros((nseq, 8 - wdt, ch), f32), buf], axis=1)


def _gla_state_in(s0):
    nseq = s0.shape[0]
    st = jnp.zeros((nseq, GLA_HEADS, GLA_DV, GLA_HEADS, GLA_DK), f32)
    for h in range(GLA_HEADS):
        st = st.at[:, h, :, h, :].set(jnp.swapaxes(s0[:, h], 1, 2))
    return st.reshape(nseq, GLA_HEADS * GLA_DV, GLA_HEADS * GLA_DK)


def _gla_state_out(st):
    nseq = st.shape[0]
    st5 = st.reshape(nseq, GLA_HEADS, GLA_DV, GLA_HEADS, GLA_DK)
    return jnp.stack([jnp.swapaxes(st5[:, h, :, h, :], 1, 2) for h in range(GLA_HEADS)], axis=1)


def _layer(x3, pe4, layer, ssm0, conv0, gla0, ffn0, lw, cfg, cache=None):
    nseq, lp, _ = x3.shape
    c, valid, ts, nsb = cfg["c"], cfg["valid"], cfg["ts"], cfg["nsb"]
    rows = nseq * lp
    p_ssd, p_gla, p_mla = _in_proj(x3.reshape(rows, D_MODEL), lw["w_in"], min(rows, 512))
    y_ssd, ssm_h, conv_halo = _ssd(p_ssd.reshape(nseq, lp, P_SSD), _halo(conv0), ssm0, lw, c, valid)
    y_gla, gla_st = _gla(p_gla.reshape(nseq, lp, P_GLA), _gla_state_in(gla0), lw, c, valid)
    qcat, kcat, c_kv, k_rope = _mla_prep(p_mla.reshape(nseq, lp, P_MLA), cfg["cos"], cfg["sin"], lw, nsb, ts)
    if cache is None:
        y_mla = _mla_flash(qcat, kcat, lw, ts)
    else:
        cache_kv, cache_krt, page_table = cache
        y_mla = _mla_paged(qcat, kcat, cache_kv, cache_krt, page_table, layer, lw, valid)
    x_out, ffn_halo = _ffn(x3, y_ssd, y_gla, y_mla, pe4, layer, _halo(ffn0), lw, nsb, ts, valid)
    states = (c_kv[:, :valid], k_rope[:, :valid], ssm_h, conv_halo[:, 8 - (SSD_CONV - 1):],
              _gla_state_out(gla_st), ffn_halo[:, 8 - (FFN_CONV - 1):])
    return x_out, states


def _pad_rows(a, lp):
    nseq, l = a.shape[:2]
    if l == lp:
        return a
    return jnp.concatenate([a, jnp.zeros((nseq, lp - l) + a.shape[2:], a.dtype)], axis=1)


def kernel(x_prompt, x_sample, cache_kv_latent, cache_k_rope, state_ssm, state_ssm_conv, state_gla, state_ffn_conv, page_table, p_prompt, p_sample, ln_in_g, ln_in_b, w_in, ssd_conv_w, ssd_conv_b, ssd_dt_bias, ssd_a_log, ssd_d, ssd_norm_g, gla_gate_w, gla_gate_b, gla_norm_g, mla_q_norm_g, mla_w_uq, mla_kv_norm_g, mla_w_uk, mla_w_uv, w_out, ln1_g, ln1_b, ffn_w_up, ffn_conv_w, ffn_conv_b, ffn_w_down, ple_w_proj, ple_w_gate, ln2_g, ln2_b):
    weights = dict(w_in=w_in, ssd_conv_w=ssd_conv_w, ssd_conv_b=ssd_conv_b, ssd_dt_bias=ssd_dt_bias,
                   ssd_a_log=ssd_a_log, ssd_d=ssd_d, ssd_norm_g=ssd_norm_g, gla_gate_w=gla_gate_w,
                   gla_gate_b=gla_gate_b, gla_norm_g=gla_norm_g, mla_q_norm_g=mla_q_norm_g, mla_w_uq=mla_w_uq,
                   mla_kv_norm_g=mla_kv_norm_g, mla_w_uk=mla_w_uk, mla_w_uv=mla_w_uv, w_out=w_out, ln1_g=ln1_g,
                   ln1_b=ln1_b, ffn_w_up=ffn_w_up, ffn_conv_w=ffn_conv_w, ffn_conv_b=ffn_conv_b,
                   ffn_w_down=ffn_w_down, ple_w_proj=ple_w_proj, ple_w_gate=ple_w_gate, ln2_g=ln2_g, ln2_b=ln2_b)
    depth = w_in.shape[0]
    bp, lp_, _ = x_prompt.shape
    bs, ls, _ = x_sample.shape
    t_past = page_table.shape[1] * PAGE_SIZE

    g_in = ln_in_g.reshape(1, -1)
    b_in = ln_in_b.reshape(1, -1)
    hp = _ln_in(x_prompt.reshape(bp * lp_, D_MODEL), g_in, b_in, min(bp * lp_, 1024)).reshape(bp, lp_, D_MODEL)
    xs_pad = _pad_rows(x_sample, SEG_PAD)
    hs = _ln_in(xs_pad.reshape(bs * SEG_PAD, D_MODEL), g_in, b_in, bs * SEG_PAD).reshape(bs, SEG_PAD, D_MODEL)

    ts_p = min(lp_, 512)
    cos_p, sin_p = _rope_tables(jnp.arange(lp_), ts_p)
    cfg_p = dict(c=min(lp_, 128), valid=lp_, ts=ts_p, nsb=1, cos=cos_p, sin=sin_p)
    cos_s, sin_s = _rope_tables(t_past + jnp.arange(SEG_PAD), bs * SEG_PAD)
    cfg_s = dict(c=SEG_PAD, valid=ls, ts=SEG_PAD, nsb=bs, cos=cos_s, sin=sin_s)

    ssm0 = jnp.zeros((bp, SSD_HEADS, HEAD_DIM, SSD_STATE), f32)
    conv0 = jnp.zeros((bp, SSD_CONV - 1, SSD_CONV_CH), f32)
    gla0 = jnp.zeros((bp, GLA_HEADS, GLA_DK, GLA_DV), f32)
    ffn0 = jnp.zeros((bp, FFN_CONV - 1, D_FF), f32)

    cache_krt = jnp.swapaxes(cache_k_rope, 2, 3)

    pe_s = jnp.concatenate([p_sample, jnp.zeros((depth, bs, SEG_PAD - ls, PLE_DIM), f32)], axis=2)

    prompt_states, sample_states = [], []
    for i in range(depth):
        lw = _prep_layer_weights(i, weights)
        hp, sp = _layer(hp, p_prompt, i, ssm0, conv0, gla0, ffn0, lw, cfg_p)
        hs, ss = _layer(hs, pe_s, i, state_ssm[i], state_ssm_conv[i], state_gla[i],
                        state_ffn_conv[i], lw, cfg_s, cache=(cache_kv_latent, cache_krt, page_table))
        prompt_states.append(sp)
        sample_states.append(ss)
    kv_p, kr_p, ssm_p, ssm_conv_p, gla_p, ffn_conv_p = (jnp.stack(f) for f in zip(*prompt_states))
    kv_s, kr_s, ssm_s, ssm_conv_s, gla_s, ffn_conv_s = (jnp.stack(f) for f in zip(*sample_states))
    return (hp, hs[:, :ls], kv_p, kv_s, kr_p, kr_s, ssm_p, ssm_s, ssm_conv_p, ssm_conv_s, gla_p, gla_s,
            ffn_conv_p, ffn_conv_s)
```

```python
import functools
import math

import jax
import jax.numpy as jnp
import numpy as np
from jax import lax
from jax.experimental import pallas as pl
from jax.experimental.pallas import tpu as pltpu

f32 = jnp.float32
bf16 = jnp.bfloat16

D_MODEL = 1024
HEAD_DIM = 64
SSD_WIDTH = 512
SSD_HEADS = 8
SSD_GROUPS = 2
SSD_STATE = 128
SSD_CONV = 4
SSD_CONV_CH = SSD_WIDTH + 2 * SSD_GROUPS * SSD_STATE
GLA_WIDTH = 256
GLA_HEADS = 4
GLA_DK = 32
GLA_DV = 64
GLA_GATE_RANK = 16
GLA_TAU = 16.0
MLA_HEADS = 4
MLA_NOPE = 64
MLA_ROPE = 32
MLA_DV = 64
MLA_Q_LORA = 192
MLA_KV_LORA = 128
MLA_SCALE = (MLA_NOPE + MLA_ROPE) ** -0.5
EXP2_SCALE = MLA_SCALE * math.log2(math.e)
ROPE_THETA = 10000.0
D_FF = 2816
FFN_CONV = 3
PLE_DIM = 256
PAGE_SIZE = 128
DEPTH = 4
DN_ALPHA = (2 * DEPTH) ** 0.25

P_SSD = 1664
P_GLA = 896
P_MLA = 640
QK_W = 256

SEG_PAD = 8
FF_CHUNK = 256
PAGED_SLOTS = 3
PAGED_CHUNK = 2048
GLA_SAFE_LOG_DECAY = -60.0
VMEM_LIMIT = 60 * 1024 * 1024


def _bdot(a, b):
    return jnp.dot(a.astype(bf16), b.astype(bf16), preferred_element_type=f32)


def _bdot_nt(a, b):
    return lax.dot_general(a.astype(bf16), b.astype(bf16), (((1,), (1,)), ((), ())),
                           preferred_element_type=f32)


def _bdot_tn(a, b):
    return lax.dot_general(a.astype(bf16), b.astype(bf16), (((0,), (0,)), ((), ())),
                           preferred_element_type=f32)


def _split3(a):
    hi = a.astype(bf16)
    r1 = a - hi.astype(f32)
    mid = r1.astype(bf16)
    lo = (r1 - mid.astype(f32)).astype(bf16)
    return hi, mid, lo


def _dot_f32_lhs(a, m):
    m = m.astype(bf16)
    hi, mid, lo = _split3(a)
    return (jnp.dot(hi, m, preferred_element_type=f32) + jnp.dot(mid, m, preferred_element_type=f32)
            + jnp.dot(lo, m, preferred_element_type=f32))


def _dot_f32_rhs(m, a):
    m = m.astype(bf16)
    hi, mid, lo = _split3(a)
    return (jnp.dot(m, hi, preferred_element_type=f32) + jnp.dot(m, mid, preferred_element_type=f32)
            + jnp.dot(m, lo, preferred_element_type=f32))


def _dot_split2(a, b):
    a_hi = a.astype(bf16)
    a_lo = (a - a_hi.astype(f32)).astype(bf16)
    b_hi = b.astype(bf16)
    b_lo = (b - b_hi.astype(f32)).astype(bf16)
    return (jnp.dot(a_hi, b_hi, preferred_element_type=f32) + jnp.dot(a_hi, b_lo, preferred_element_type=f32)
            + jnp.dot(a_lo, b_hi, preferred_element_type=f32))


def _silu(x):
    return x * jax.nn.sigmoid(x)


def _softplus(x):
    return jnp.maximum(x, 0.0) + jnp.log1p(jnp.exp(-jnp.abs(x)))


def _log_sigmoid(x):
    return jnp.minimum(x, 0.0) - jnp.log1p(jnp.exp(-jnp.abs(x)))


def _layernorm(x, g, b, eps=1e-5):
    xc = x - jnp.mean(x, -1, keepdims=True)
    var = jnp.mean(xc * xc, -1, keepdims=True)
    return xc * lax.rsqrt(var + eps) * g + b


def _tri(c):
    r = lax.broadcasted_iota(jnp.int32, (c, c), 0)
    col = lax.broadcasted_iota(jnp.int32, (c, c), 1)
    return col <= r


def _params(sem):
    return pltpu.CompilerParams(dimension_semantics=sem, vmem_limit_bytes=VMEM_LIMIT)


def _const_spec(shape):
    nd = len(shape)
    return pl.BlockSpec(shape, lambda *_: (0,) * nd)


def _layer_spec(arr, layer):
    shape = arr.shape[1:]
    nd = len(shape)
    return pl.BlockSpec((None,) + shape, lambda *_: (layer,) + (0,) * nd, pipeline_mode=pl.Buffered(1))


def _ln_kernel(x_ref, g_ref, b_ref, o_ref):
    o_ref[...] = _layernorm(x_ref[...], g_ref[...], b_ref[...])


def _ln_in(x2, g, b, tm):
    rows = x2.shape[0]
    return pl.pallas_call(
        _ln_kernel,
        grid=(rows // tm,),
        in_specs=[pl.BlockSpec((tm, D_MODEL), lambda i: (i, 0)), _const_spec((1, D_MODEL)), _const_spec((1, D_MODEL))],
        out_specs=pl.BlockSpec((tm, D_MODEL), lambda i: (i, 0)),
        out_shape=jax.ShapeDtypeStruct((rows, D_MODEL), f32),
        compiler_params=_params(("parallel",)),
        name="ln_in",
    )(x2, g, b)


def _in_proj_kernel(x_ref, w_ref, ssd_ref, gla_ref, mla_ref):
    x = x_ref[...].astype(bf16)
    ssd_ref[...] = jnp.dot(x, w_ref[:, 0:P_SSD], preferred_element_type=f32)
    gla_ref[...] = jnp.dot(x, w_ref[:, P_SSD:P_SSD + P_GLA], preferred_element_type=f32)
    mla_ref[...] = jnp.dot(x, w_ref[:, P_SSD + P_GLA:], preferred_element_type=f32)


def _in_proj(x2, lw, layer, tm):
    rows = x2.shape[0]
    return pl.pallas_call(
        _in_proj_kernel,
        grid=(rows // tm,),
        in_specs=[pl.BlockSpec((tm, D_MODEL), lambda i: (i, 0)), _layer_spec(lw["w_in"], layer)],
        out_specs=[pl.BlockSpec((tm, P_SSD), lambda i: (i, 0)),
                   pl.BlockSpec((tm, P_GLA), lambda i: (i, 0)),
                   pl.BlockSpec((tm, P_MLA), lambda i: (i, 0))],
        out_shape=[jax.ShapeDtypeStruct((rows, P_SSD), f32),
                   jax.ShapeDtypeStruct((rows, P_GLA), f32),
                   jax.ShapeDtypeStruct((rows, P_MLA), f32)],
        compiler_params=_params(("parallel",)),
        name="in_proj",
    )(x2, lw["w_in"])


def _ssd_kernel(nsq, c, valid, p_ref, halo0_ref, h0_ref, cw_ref, cb_ref, dtb_ref, alog_ref, dexp_ref, ng_ref,
                y_ref, hout_ref, halo_out_ref, ht_scr, halo_scr):
    ci = pl.program_id(1)
    hp = SSD_HEADS * HEAD_DIM
    gw = hp // SSD_GROUPS
    heads_per_group = SSD_HEADS // SSD_GROUPS

    @pl.when(ci == 0)
    def _():
        for s in range(nsq):
            ht_scr[s] = h0_ref[s].reshape(hp, SSD_STATE).T
        halo_scr[...] = halo0_ref[...]

    row = lax.broadcasted_iota(jnp.int32, (c, 1), 0)
    tri = _tri(c)
    tri_b = tri.astype(bf16)
    expand = (lax.broadcasted_iota(jnp.int32, (SSD_HEADS, hp), 1) // HEAD_DIM
              == lax.broadcasted_iota(jnp.int32, (SSD_HEADS, hp), 0)).astype(bf16)
    lane_head = lax.broadcasted_iota(jnp.int32, (1, gw), 1) // HEAD_DIM
    cw = cw_ref[...]
    a_neg = -jnp.exp(alog_ref[...])

    for s in range(nsq):
        p = p_ref[s]
        z = p[:, 0:SSD_WIDTH]
        xr = p[:, SSD_WIDTH:SSD_WIDTH + SSD_CONV_CH]
        dt_raw = p[:, SSD_WIDTH + SSD_CONV_CH:SSD_WIDTH + SSD_CONV_CH + SSD_HEADS]

        halo = halo_scr[s]
        hm1, hm2, hm3 = halo[7:8, :], halo[6:7, :], halo[5:6, :]
        s1 = jnp.where(row >= 1, pltpu.roll(xr, 1, axis=0), hm1)
        s2 = jnp.where(row >= 2, pltpu.roll(xr, 2, axis=0), jnp.where(row == 1, hm1, hm2))
        s3 = jnp.where(row >= 3, pltpu.roll(xr, 3, axis=0),
                       jnp.where(row == 2, hm1, jnp.where(row == 1, hm2, hm3)))
        conv = cw[3:4, :] * xr + cw[2:3, :] * s1 + cw[1:2, :] * s2 + cw[0:1, :] * s3 + cb_ref[...]
        last = xr[c - 8:c, :]
        if valid < c:
            last = pltpu.roll(last, c - valid, axis=0)
        halo_scr[s] = last

        xbc = _silu(conv)
        xs = xbc[:, 0:SSD_WIDTH]
        bs = xbc[:, SSD_WIDTH:SSD_WIDTH + SSD_GROUPS * SSD_STATE]
        cs = xbc[:, SSD_WIDTH + SSD_GROUPS * SSD_STATE:]

        dt = _softplus(dt_raw + dtb_ref[...])
        if valid < c:
            dt = jnp.where(row < valid, dt, 0.0)
        la = dt * a_neg
        cum = _dot_f32_rhs(tri_b, la)
        cum_t = cum.T
        dt_t = dt.T
        w = jnp.exp(cum[c - 1:c, :] - cum) * dt
        ecum_x = _dot_f32_lhs(jnp.exp(cum), expand)
        w_x = _dot_f32_lhs(w, expand)

        ht = ht_scr[s]
        y_parts = []
        for g in range(SSD_GROUPS):
            sl = slice(g * gw, (g + 1) * gw)
            b_g = bs[:, g * SSD_STATE:(g + 1) * SSD_STATE]
            c_g = cs[:, g * SSD_STATE:(g + 1) * SSD_STATE]
            x_g = xs[:, sl]
            scores = _bdot_nt(c_g, b_g)
            y_g = _bdot(c_g, ht[:, sl]) * ecum_x[:, sl]
            for hh in range(heads_per_group):
                h = g * heads_per_group + hh
                seg = cum[:, h:h + 1] - cum_t[h:h + 1, :]
                decay = jnp.exp(jnp.where(tri, seg, -jnp.inf))
                m = scores * decay * dt_t[h:h + 1, :]
                xm = jnp.where(lane_head == hh, x_g, 0.0)
                y_g = y_g + _bdot(m, xm)
            y_parts.append(y_g)
            ht_scr[s, :, sl] = ht[:, sl] * ecum_x[c - 1:c, sl] + _bdot_tn(b_g, x_g * w_x[:, sl])
        y = jnp.concatenate(y_parts, axis=-1)
        y = (y + dexp_ref[...] * xs) * _silu(z)
        ng = ng_ref[...]
        outs = []
        for g in range(SSD_GROUPS):
            sl = slice(g * gw, (g + 1) * gw)
            yg = y[:, sl]
            outs.append(yg * lax.rsqrt(jnp.mean(yg * yg, -1, keepdims=True) + 1e-6) * ng[:, sl])
        y_ref[s] = jnp.concatenate(outs, axis=-1).astype(bf16)

    @pl.when(ci == pl.num_programs(1) - 1)
    def _():
        for s in range(nsq):
            hout_ref[s] = ht_scr[s].T.reshape(SSD_HEADS, HEAD_DIM, SSD_STATE)
        halo_out_ref[...] = halo_scr[...]


def _ssd(p_ssd, halo0, h0, st_layer, lw, layer, nsq, c, valid):
    nseq, lp, _ = p_ssd.shape
    hp = SSD_HEADS * HEAD_DIM
    return pl.pallas_call(
        functools.partial(_ssd_kernel, nsq, c, valid),
        grid=(nseq // nsq, lp // c),
        in_specs=[pl.BlockSpec((nsq, c, P_SSD), lambda b, i: (b, i, 0)),
                  pl.BlockSpec((None, nsq, 8, SSD_CONV_CH), lambda b, i: (st_layer, b, 0, 0)),
                  pl.BlockSpec((None, nsq, SSD_HEADS, HEAD_DIM, SSD_STATE), lambda b, i: (st_layer, b, 0, 0, 0)),
                  _layer_spec(lw["ssd_conv_w"], layer), _layer_spec(lw["ssd_conv_b"], layer),
                  _layer_spec(lw["ssd_dt_bias"], layer), _layer_spec(lw["ssd_a_log"], layer),
                  _layer_spec(lw["ssd_d_x"], layer), _layer_spec(lw["ssd_norm_g"], layer)],
        out_specs=[pl.BlockSpec((nsq, c, hp), lambda b, i: (b, i, 0)),
                   pl.BlockSpec((nsq, SSD_HEADS, HEAD_DIM, SSD_STATE), lambda b, i: (b, 0, 0, 0)),
                   pl.BlockSpec((nsq, 8, SSD_CONV_CH), lambda b, i: (b, 0, 0))],
        out_shape=[jax.ShapeDtypeStruct((nseq, lp, hp), bf16),
                   jax.ShapeDtypeStruct((nseq, SSD_HEADS, HEAD_DIM, SSD_STATE), f32),
                   jax.ShapeDtypeStruct((nseq, 8, SSD_CONV_CH), f32)],
        scratch_shapes=[pltpu.VMEM((nsq, SSD_STATE, hp), f32), pltpu.VMEM((nsq, 8, SSD_CONV_CH), f32)],
        compiler_params=_params(("parallel", "arbitrary")),
        name="ssd",
    )(p_ssd, halo0, h0, lw["ssd_conv_w"], lw["ssd_conv_b"], lw["ssd_dt_bias"], lw["ssd_a_log"],
      lw["ssd_d_x"], lw["ssd_norm_g"])


def _gla_kernel(nsq, c, valid, p_ref, st0_ref, gw_ref, gb_ref, gn_ref, y_ref, st_out_ref, st_scr, q_scr, b_scr,
                o_scr):
    ci = pl.program_id(1)
    hk = GLA_HEADS * GLA_DK
    hv = GLA_HEADS * GLA_DV
    blk = (lax.broadcasted_iota(jnp.int32, (hk, hv), 0) // GLA_DK
           == lax.broadcasted_iota(jnp.int32, (hk, hv), 1) // GLA_DV)

    @pl.when(ci == 0)
    def _():
        for s in range(nsq):
            s2d = st0_ref[s].reshape(hk, GLA_DV)
            st_scr[s] = jnp.where(blk, jnp.concatenate([s2d] * GLA_HEADS, axis=-1), 0.0)

    row = lax.broadcasted_iota(jnp.int32, (c, 1), 0)
    tri = _tri(c)
    tri_b = tri.astype(bf16)
    lane_k = lax.broadcasted_iota(jnp.int32, (1, hk), 1) // GLA_DK
    lane_v = lax.broadcasted_iota(jnp.int32, (1, hv), 1) // GLA_DV
    head_sum = blk.astype(bf16)
    gmat = jnp.where(lax.broadcasted_iota(jnp.int32, (hv, hv), 0) // GLA_DV
                     == lax.broadcasted_iota(jnp.int32, (hv, hv), 1) // GLA_DV, 1.0 / GLA_DV, 0.0).astype(bf16)

    seqs = []
    for s in range(nsq):
        p = p_ref[s]
        q = p[:, 0:hk] * GLA_DK ** -0.5
        k = p[:, hk:2 * hk]
        v = p[:, 2 * hk:2 * hk + hv]
        glr = p[:, 2 * hk + 2 * hv:2 * hk + 2 * hv + GLA_GATE_RANK]
        log_a = _log_sigmoid(_dot_split2(glr, gw_ref[...]) + gb_ref[...]) / GLA_TAU
        if valid < c:
            log_a = jnp.where(row < valid, log_a, 0.0)
            k = jnp.where(row < valid, k, 0.0)
        b = _dot_f32_rhs(tri_b, log_a)
        b_last = b[c - 1:c, :]
        qt = q * jnp.exp(b)
        khat = k * jnp.exp(b_last - b)
        st = st_scr[s]
        o_inter = _bdot(qt, st)
        e_col = jnp.exp(jnp.broadcast_to(b_last, (8, hk)).T[:, 0:1])
        st_scr[s] = jnp.where(blk, st * e_col + _bdot_tn(khat, v), 0.0)
        seqs.append((q, k, v, b, qt, o_inter, b_last))

    def intra_fast():
        outs = []
        for (q, k, v, b, qt, _, _) in seqs:
            kt = k * jnp.exp(-b)
            o = jnp.zeros((c, hv), f32)
            for h in range(GLA_HEADS):
                att = _bdot_nt(jnp.where(lane_k == h, qt, 0.0), kt)
                att = jnp.where(tri, att, 0.0)
                o = o + _bdot(att, jnp.where(lane_v == h, v, 0.0))
            outs.append(o)
        return outs

    def intra_exact():
        outs = []
        for (q, k, v, b, _, _, _) in seqs:
            q_scr[...] = q
            b_scr[...] = b

            def body(i, carry, k=k, v=v, b=b):
                qi = q_scr[pl.ds(i, 1), :]
                bi = b_scr[pl.ds(i, 1), :]
                t = jnp.where(row <= i, qi * k * jnp.exp(jnp.minimum(bi - b, 0.0)), 0.0)
                wgt = jnp.dot(t.astype(bf16), head_sum, preferred_element_type=f32)
                o_scr[pl.ds(i, 1), :] = jnp.sum(wgt * v, axis=0, keepdims=True)
                return carry

            lax.fori_loop(0, c, body, 0)
            outs.append(o_scr[...])
        return outs

    min_decay = seqs[0][6]
    for sq in seqs[1:]:
        min_decay = jnp.minimum(min_decay, sq[6])
    intra = lax.cond(jnp.min(min_decay) > GLA_SAFE_LOG_DECAY, intra_fast, intra_exact)

    for s in range(nsq):
        o = seqs[s][5] + intra[s]
        r = p_ref[s][:, 2 * hk + hv:2 * hk + 2 * hv]
        oo = o * o
        oo_hi = oo.astype(bf16)
        oo_lo = (oo - oo_hi.astype(f32)).astype(bf16)
        ms = jnp.dot(oo_hi, gmat, preferred_element_type=f32) + jnp.dot(oo_lo, gmat, preferred_element_type=f32)
        y = o * lax.rsqrt(ms + 1e-6) * gn_ref[...] * _silu(r)
        y_ref[s] = y.astype(bf16)

    @pl.when(ci == pl.num_programs(1) - 1)
    def _():
        for s in range(nsq):
            st = st_scr[s]
            out = st[:, 0:GLA_DV]
            for h in range(1, GLA_HEADS):
                out = out + st[:, h * GLA_DV:(h + 1) * GLA_DV]
            st_out_ref[s] = out.reshape(GLA_HEADS, GLA_DK, GLA_DV)


def _gla(p_gla, st0, st_layer, lw, layer, nsq, c, valid):
    nseq, lp, _ = p_gla.shape
    hk = GLA_HEADS * GLA_DK
    hv = GLA_HEADS * GLA_DV
    return pl.pallas_call(
        functools.partial(_gla_kernel, nsq, c, valid),
        grid=(nseq // nsq, lp // c),
        in_specs=[pl.BlockSpec((nsq, c, P_GLA), lambda b, i: (b, i, 0)),
                  pl.BlockSpec((None, nsq, GLA_HEADS, GLA_DK, GLA_DV), lambda b, i: (st_layer, b, 0, 0, 0)),
                  _layer_spec(lw["gla_gate_w"], layer), _layer_spec(lw["gla_gate_b"], layer),
                  _layer_spec(lw["gla_norm_g_x"], layer)],
        out_specs=[pl.BlockSpec((nsq, c, hv), lambda b, i: (b, i, 0)),
                   pl.BlockSpec((nsq, GLA_HEADS, GLA_DK, GLA_DV), lambda b, i: (b, 0, 0, 0))],
        out_shape=[jax.ShapeDtypeStruct((nseq, lp, hv), bf16),
                   jax.ShapeDtypeStruct((nseq, GLA_HEADS, GLA_DK, GLA_DV), f32)],
        scratch_shapes=[pltpu.VMEM((nsq, hk, hv), f32), pltpu.VMEM((c, hk), f32), pltpu.VMEM((c, hk), f32),
                        pltpu.VMEM((c, hv), f32)],
        compiler_params=_params(("parallel", "arbitrary")),
        name="gla",
    )(p_gla, st0, lw["gla_gate_w"], lw["gla_gate_b"], lw["gla_norm_g_x"])


def _mla_prep_kernel(nsb, ts, p_ref, cos_ref, sin_ref, qg_ref, kvg_ref, wuq_ref, wcat_ref, sel_ref,
                     qcat_ref, kcat_ref, ckv_ref, krope_ref):
    rows = nsb * ts
    p = p_ref[...].reshape(rows, P_MLA)
    cq = p[:, 0:256]
    ckv = p[:, 256:384]
    kra = p[:, 384:512]
    krb = p[:, 512:640]
    cos = cos_ref[...]
    sin = sin_ref[...]
    cqn = cq * lax.rsqrt(jnp.sum(cq * cq, -1, keepdims=True) * (1.0 / MLA_Q_LORA) + 1e-6) * qg_ref[...]
    qf = _bdot(cqn, wuq_ref[...])
    ra = qf[:, 256:384]
    rb = qf[:, 384:512]
    feat = jnp.concatenate([qf[:, 0:256], ra * cos - rb * sin, ra * sin + rb * cos], axis=-1).astype(bf16)
    for h in range(MLA_HEADS):
        qh = jnp.dot(feat, wcat_ref[h], preferred_element_type=f32).astype(bf16)
        qcat_ref[:, h] = qh.reshape(nsb, ts, QK_W)
    c_kv = ckv * lax.rsqrt(jnp.mean(ckv * ckv, -1, keepdims=True) + 1e-6) * kvg_ref[...]
    kr2 = jnp.concatenate([kra * cos - krb * sin, kra * sin + krb * cos], axis=-1)
    kr128 = _dot_f32_lhs(kr2, sel_ref[...])
    ckv_ref[...] = c_kv.reshape(nsb, ts, MLA_KV_LORA)
    krope_ref[...] = kr128[:, 0:MLA_ROPE].reshape(nsb, ts, MLA_ROPE)
    one_lane = (lax.broadcasted_iota(jnp.int32, (1, 128), 1) == 127).astype(f32)
    kcat_ref[...] = jnp.concatenate([c_kv, kr128 + one_lane], axis=-1).astype(bf16).reshape(nsb, ts, QK_W)


def _mla_prep(p_mla, cos, sin, lw, layer, nsb, ts):
    nseq, lp, _ = p_mla.shape
    rows = nsb * ts
    ntile = lp // ts
    tbl_tiles = cos.shape[0] // rows
    return pl.pallas_call(
        functools.partial(_mla_prep_kernel, nsb, ts),
        grid=(nseq // nsb, ntile),
        in_specs=[pl.BlockSpec((nsb, ts, P_MLA), lambda b, i: (b, i, 0)),
                  pl.BlockSpec((rows, 128), lambda b, i: (i % tbl_tiles, 0)),
                  pl.BlockSpec((rows, 128), lambda b, i: (i % tbl_tiles, 0)),
                  _layer_spec(lw["mla_q_norm_g_x"], layer), _layer_spec(lw["mla_kv_norm_g"], layer),
                  _layer_spec(lw["mla_w_uq_x"], layer), _layer_spec(lw["mla_wcat"], layer),
                  _const_spec((256, 128))],
        out_specs=[pl.BlockSpec((nsb, MLA_HEADS, ts, QK_W), lambda b, i: (b, 0, i, 0)),
                   pl.BlockSpec((nsb, ts, QK_W), lambda b, i: (b, i, 0)),
                   pl.BlockSpec((nsb, ts, MLA_KV_LORA), lambda b, i: (b, i, 0)),
                   pl.BlockSpec((nsb, ts, MLA_ROPE), lambda b, i: (b, i, 0))],
        out_shape=[jax.ShapeDtypeStruct((nseq, MLA_HEADS, lp, QK_W), bf16),
                   jax.ShapeDtypeStruct((nseq, lp, QK_W), bf16),
                   jax.ShapeDtypeStruct((nseq, lp, MLA_KV_LORA), f32),
                   jax.ShapeDtypeStruct((nseq, lp, MLA_ROPE), f32)],
        compiler_params=_params(("parallel", "parallel")),
        name="mla_prep",
    )(p_mla, cos, sin, lw["mla_q_norm_g_x"], lw["mla_kv_norm_g"], lw["mla_w_uq_x"], lw["mla_wcat"],
      lw["mla_sel"])


def _mla_flash_kernel(tq, tk, q_ref, k_ref, wuv_ref, y_ref, m_scr, acc_scr):
    qi = pl.program_id(1)
    ki = pl.program_id(2)

    @pl.when(ki == 0)
    def _():
        m_scr[...] = jnp.full(m_scr.shape, -jnp.inf, f32)
        acc_scr[...] = jnp.zeros(acc_scr.shape, f32)

    def step(masked):
        k = k_ref[0]
        for h in range(MLA_HEADS):
            s = lax.dot_general(q_ref[0, h], k, (((1,), (1,)), ((), ())), preferred_element_type=f32)
            if masked:
                rpos = lax.broadcasted_iota(jnp.int32, (tq, tk), 0)
                cpos = lax.broadcasted_iota(jnp.int32, (tq, tk), 1)
                s = jnp.where(cpos <= rpos, s, -jnp.inf)
            m_prev = m_scr[h]
            m_new = jnp.maximum(m_prev, jnp.max(s, -1, keepdims=True))
            alpha = jnp.exp2((m_prev - m_new) * EXP2_SCALE)
            pr = jnp.exp2((s - m_new) * EXP2_SCALE)
            acc_scr[h] = alpha * acc_scr[h] + jnp.dot(pr.astype(bf16), k, preferred_element_type=f32)
            m_scr[h] = m_new

    @pl.when(ki < qi)
    def _():
        step(False)

    @pl.when(ki == qi)
    def _():
        step(True)
        y = jnp.zeros((tq, MLA_HEADS * MLA_DV), f32)
        for h in range(MLA_HEADS):
            acc = acc_scr[h]
            y = y + _bdot(acc[:, 0:MLA_KV_LORA] / acc[:, QK_W - 1:QK_W], wuv_ref[h])
        y_ref[0] = y.astype(bf16)


def _mla_flash(qcat, kcat, lw, layer, tq):
    nseq, _, lp, _ = qcat.shape
    tk = tq
    nq = lp // tq
    return pl.pallas_call(
        functools.partial(_mla_flash_kernel, tq, tk),
        grid=(nseq, nq, nq),
        in_specs=[pl.BlockSpec((1, MLA_HEADS, tq, QK_W), lambda b, i, j: (b, 0, i, 0)),
                  pl.BlockSpec((1, tk, QK_W), lambda b, i, j: (b, jnp.minimum(i, j), 0)),
                  _layer_spec(lw["mla_wuv_x"], layer)],
        out_specs=pl.BlockSpec((1, tq, MLA_HEADS * MLA_DV), lambda b, i, j: (b, i, 0)),
        out_shape=jax.ShapeDtypeStruct((nseq, lp, MLA_HEADS * MLA_DV), bf16),
        scratch_shapes=[pltpu.VMEM((MLA_HEADS, tq, 1), f32), pltpu.VMEM((MLA_HEADS, tq, QK_W), f32)],
        compiler_params=_params(("parallel", "parallel", "arbitrary")),
        name="mla_flash",
    )(qcat, kcat, lw["mla_wuv_x"])


def _mla_paged_kernel(valid, n_pages, layer, pt_ref, q_ref, knew_ref, wuv_ref, kv_hbm, krt_hbm, y_ref,
                      kv_buf, krt_buf, kvb_scr, s_scr, sem):
    b = pl.program_id(0)
    nb = pl.num_programs(0)
    m_rows = MLA_HEADS * SEG_PAD
    t_past = n_pages * PAGE_SIZE
    chunk = math.gcd(t_past, PAGED_CHUNK)

    def page_copies(seq, i, slot):
        pg = pt_ref[seq, i]
        rows = pl.ds(i * PAGE_SIZE, PAGE_SIZE)
        return (pltpu.make_async_copy(kv_hbm.at[layer, pg], kv_buf.at[slot, rows, :], sem.at[0, slot]),
                pltpu.make_async_copy(krt_hbm.at[layer, pg], krt_buf.at[slot, :, rows], sem.at[1, slot]))

    def start_seq(seq, slot):
        for i in range(n_pages):
            for cp in page_copies(seq, i, slot):
                cp.start()

    def wait_seq(seq, slot):
        for i in range(n_pages):
            for cp in page_copies(seq, i, slot):
                cp.wait()

    @pl.when(b == 0)
    def _():
        start_seq(0, 0)
        start_seq(jnp.minimum(1, nb - 1), 1)

    slot = b % PAGED_SLOTS
    wait_seq(b, slot)
    start_seq(jnp.minimum(b + 2, nb - 1), (b + 2) % PAGED_SLOTS)

    q = q_ref[0].reshape(m_rows, QK_W)
    q_lat = q[:, 0:MLA_KV_LORA]
    q_rope = q[:, MLA_KV_LORA:MLA_KV_LORA + MLA_ROPE]
    for ci in range(t_past // chunk):
        cols = pl.ds(ci * chunk, chunk)
        kvc = kv_buf[slot, cols, :].astype(bf16)
        kvb_scr[cols, :] = kvc
        s_scr[:, cols] = (lax.dot_general(q_lat, kvc, (((1,), (1,)), ((), ())), preferred_element_type=f32)
                          + jnp.dot(q_rope, krt_buf[slot, :, cols].astype(bf16), preferred_element_type=f32))

    knew = knew_ref[0]
    sn = lax.dot_general(q, knew, (((1,), (1,)), ((), ())), preferred_element_type=f32)
    rpos = lax.broadcasted_iota(jnp.int32, (MLA_HEADS, SEG_PAD, SEG_PAD), 1).reshape(m_rows, SEG_PAD)
    cpos = lax.broadcasted_iota(jnp.int32, (m_rows, SEG_PAD), 1)
    sn = jnp.where((cpos <= rpos) & (cpos < valid), sn, -jnp.inf)
    m = jnp.maximum(jnp.max(s_scr[...], -1, keepdims=True), jnp.max(sn, -1, keepdims=True))
    pn = jnp.exp((sn - m) * MLA_SCALE)
    l = jnp.sum(pn, -1, keepdims=True)
    acc = jnp.dot(pn.astype(bf16), knew[:, 0:MLA_KV_LORA], preferred_element_type=f32)
    for ci in range(t_past // chunk):
        cols = pl.ds(ci * chunk, chunk)
        pr = jnp.exp((s_scr[:, cols] - m) * MLA_SCALE)
        l = l + jnp.sum(pr, -1, keepdims=True)
        acc = acc + jnp.dot(pr.astype(bf16), kvb_scr[cols, :], preferred_element_type=f32)
    o_lat = acc / l
    y = jnp.zeros((SEG_PAD, MLA_HEADS * MLA_DV), f32)
    for h in range(MLA_HEADS):
        y = y + _bdot(o_lat[h * SEG_PAD:(h + 1) * SEG_PAD, :], wuv_ref[h])
    y_ref[0] = y.astype(bf16)

    @pl.when(b == nb - 1)
    def _():
        wait_seq(b, (b + 1) % PAGED_SLOTS)
        wait_seq(b, (b + 2) % PAGED_SLOTS)


def _mla_paged(qcat, kcat, cache_kv, cache_krt, page_table, lw, layer, valid):
    nseq = qcat.shape[0]
    n_pages = page_table.shape[1]
    t_past = n_pages * PAGE_SIZE
    wuv = lw["mla_wuv_x"]
    grid_spec = pltpu.PrefetchScalarGridSpec(
        num_scalar_prefetch=1,
        grid=(nseq,),
        in_specs=[pl.BlockSpec((1, MLA_HEADS, SEG_PAD, QK_W), lambda b, pt: (b, 0, 0, 0)),
                  pl.BlockSpec((1, SEG_PAD, QK_W), lambda b, pt: (b, 0, 0)),
                  pl.BlockSpec((None,) + wuv.shape[1:], lambda b, pt: (layer, 0, 0, 0)),
                  pl.BlockSpec(memory_space=pl.ANY), pl.BlockSpec(memory_space=pl.ANY)],
        out_specs=pl.BlockSpec((1, SEG_PAD, MLA_HEADS * MLA_DV), lambda b, pt: (b, 0, 0)),
        scratch_shapes=[pltpu.VMEM((PAGED_SLOTS, t_past, MLA_KV_LORA), f32),
                        pltpu.VMEM((PAGED_SLOTS, MLA_ROPE, t_past), f32),
                        pltpu.VMEM((t_past, MLA_KV_LORA), bf16),
                        pltpu.VMEM((MLA_HEADS * SEG_PAD, t_past), f32),
                        pltpu.SemaphoreType.DMA((2, PAGED_SLOTS))],
    )
    return pl.pallas_call(
        functools.partial(_mla_paged_kernel, valid, n_pages, layer),
        grid_spec=grid_spec,
        out_shape=jax.ShapeDtypeStruct((nseq, SEG_PAD, MLA_HEADS * MLA_DV), bf16),
        compiler_params=_params(("arbitrary",)),
        name="mla_paged",
    )(page_table, qcat, kcat, wuv, cache_kv, cache_krt)


def _ffn_kernel(nsb, ts, valid, x_ref, ys_ref, yg_ref, ym_ref, pe_ref, halo0_ref,
                wout_ref, g1_ref, b1_ref, wup_ref, fcw_ref, fcb_ref, wdn_ref, wpg_ref, wpp_ref, g2_ref, b2_ref,
                o_ref, halo_out_ref, halo_scr, hmid_scr):
    ti = pl.program_id(1)
    rows = nsb * ts

    @pl.when(ti == 0)
    def _():
        halo_scr[...] = halo0_ref[...]

    x = x_ref[...].reshape(rows, D_MODEL)
    ycat = jnp.concatenate([ys_ref[...].reshape(rows, SSD_WIDTH), yg_ref[...].reshape(rows, GLA_WIDTH),
                            ym_ref[...].reshape(rows, MLA_HEADS * MLA_DV)], axis=-1)
    mix = jnp.dot(ycat, wout_ref[...], preferred_element_type=f32)
    x1 = _layernorm(DN_ALPHA * x + mix, g1_ref[...], b1_ref[...])
    x1b = x1.astype(bf16)

    pos = lax.broadcasted_iota(jnp.int32, (nsb, ts, 1), 1)
    for ch in range(D_FF // FF_CHUNK):
        lo = ch * FF_CHUNK
        a = jnp.dot(x1b, wup_ref[:, lo:lo + FF_CHUNK], preferred_element_type=f32)
        bgate = jnp.dot(x1b, wup_ref[:, D_FF + lo:D_FF + lo + FF_CHUNK], preferred_element_type=f32)
        a3 = a.reshape(nsb, ts, FF_CHUNK)
        halo = halo_scr[:, :, lo:lo + FF_CHUNK]
        hm1, hm2 = halo[:, 7:8, :], halo[:, 6:7, :]
        s1 = jnp.where(pos >= 1, pltpu.roll(a3, 1, axis=1), hm1)
        s2 = jnp.where(pos >= 2, pltpu.roll(a3, 2, axis=1), jnp.where(pos == 1, hm1, hm2))
        fcw = fcw_ref[:, lo:lo + FF_CHUNK]
        ac = fcw[2:3, :] * a3 + fcw[1:2, :] * s1 + fcw[0:1, :] * s2 + fcb_ref[:, lo:lo + FF_CHUNK]
        last = a3[:, ts - 8:ts, :]
        if valid < ts:
            last = pltpu.roll(last, ts - valid, axis=1)
        halo_scr[:, :, lo:lo + FF_CHUNK] = last
        ge = 0.5 * ac * (1.0 + lax.erf(ac * (1.0 / math.sqrt(2.0))))
        hmid_scr[:, lo:lo + FF_CHUNK] = (ge.reshape(rows, FF_CHUNK) * bgate).astype(bf16)
    f = jnp.dot(hmid_scr[...], wdn_ref[...], preferred_element_type=f32)

    gate = jax.nn.sigmoid(jnp.dot(x1b, wpg_ref[...], preferred_element_type=f32))
    pe = _bdot(pe_ref[...].reshape(rows, PLE_DIM), wpp_ref[...])
    x2 = _layernorm(DN_ALPHA * x1 + f + gate * pe, g2_ref[...], b2_ref[...])
    o_ref[...] = x2.reshape(nsb, ts, D_MODEL)

    @pl.when(ti == pl.num_programs(1) - 1)
    def _():
        halo_out_ref[...] = halo_scr[...]


def _ffn(x3, ys, yg, ym, pe4, halo0, st_layer, lw, layer, nsb, ts, valid):
    nseq, lp, _ = x3.shape

    def tok(width):
        return pl.BlockSpec((nsb, ts, width), lambda b, i: (b, i, 0))

    names = ["w_out", "ln1_g", "ln1_b", "ffn_w_up", "ffn_conv_w", "ffn_conv_b", "ffn_w_down", "ple_w_gate",
             "ple_w_proj", "ln2_g", "ln2_b"]
    return pl.pallas_call(
        functools.partial(_ffn_kernel, nsb, ts, valid),
        grid=(nseq // nsb, lp // ts),
        in_specs=[tok(D_MODEL), tok(SSD_WIDTH), tok(GLA_WIDTH), tok(MLA_HEADS * MLA_DV),
                  pl.BlockSpec((None, nsb, ts, PLE_DIM), lambda b, i: (layer, b, i, 0)),
                  pl.BlockSpec((None, nsb, 8, D_FF), lambda b, i: (st_layer, b, 0, 0))]
        + [_layer_spec(lw[n], layer) for n in names],
        out_specs=[tok(D_MODEL), pl.BlockSpec((nsb, 8, D_FF), lambda b, i: (b, 0, 0))],
        out_shape=[jax.ShapeDtypeStruct((nseq, lp, D_MODEL), f32),
                   jax.ShapeDtypeStruct((nseq, 8, D_FF), f32)],
        scratch_shapes=[pltpu.VMEM((nsb, 8, D_FF), f32), pltpu.VMEM((nsb * ts, D_FF), bf16)],
        compiler_params=_params(("parallel", "arbitrary")),
        name="out_ffn",
    )(x3, ys, yg, ym, pe4, halo0, *[lw[n] for n in names])


def _prep_weights(w):
    depth = w["w_in"].shape[0]

    def row(v):
        return v.reshape(depth, 1, -1).astype(f32)

    splits = np.cumsum([0, SSD_WIDTH, SSD_CONV_CH, SSD_HEADS, GLA_HEADS * GLA_DK, GLA_HEADS * GLA_DK, GLA_WIDTH,
                        GLA_WIDTH, GLA_GATE_RANK, MLA_Q_LORA, MLA_KV_LORA, MLA_ROPE]).tolist()
    w_in = w["w_in"]
    (z, xbc, dt, q, k, v, r, glr, cq, ckv, kr) = [w_in[:, :, splits[j]:splits[j + 1]] for j in range(11)]

    def zpad(n):
        return jnp.zeros((depth, D_MODEL, n), f32)

    half = MLA_ROPE // 2
    w_all = jnp.concatenate(
        [z, xbc, dt, zpad(P_SSD - SSD_WIDTH - SSD_CONV_CH - SSD_HEADS),
         q, k, v, r, glr, zpad(P_GLA - 2 * GLA_HEADS * GLA_DK - 2 * GLA_WIDTH - GLA_GATE_RANK),
         cq, zpad(256 - MLA_Q_LORA), ckv, kr[:, :, :half], zpad(128 - half), kr[:, :, half:], zpad(128 - half)],
        axis=2).astype(bf16)

    wuq = w["mla_w_uq"].reshape(depth, MLA_Q_LORA, MLA_HEADS, MLA_NOPE + MLA_ROPE)
    nope = wuq[..., :MLA_NOPE].reshape(depth, MLA_Q_LORA, MLA_HEADS * MLA_NOPE)
    ra = wuq[..., MLA_NOPE:MLA_NOPE + half].reshape(depth, MLA_Q_LORA, MLA_HEADS * half)
    rb = wuq[..., MLA_NOPE + half:].reshape(depth, MLA_Q_LORA, MLA_HEADS * half)
    zq = jnp.zeros((depth, MLA_Q_LORA, 128 - MLA_HEADS * half), f32)
    wuq_x = jnp.concatenate([nope, ra, zq, rb, zq], axis=2)
    wuq_x = jnp.concatenate([wuq_x, jnp.zeros((depth, 256 - MLA_Q_LORA, 512), f32)], axis=1).astype(bf16)

    wuk = w["mla_w_uk"]
    wcat = jnp.zeros((depth, MLA_HEADS, 512, QK_W), f32)
    eye = jnp.eye(half, dtype=f32)
    for h in range(MLA_HEADS):
        wcat = wcat.at[:, h, h * MLA_NOPE:(h + 1) * MLA_NOPE, 0:MLA_KV_LORA].set(jnp.swapaxes(wuk[:, :, h, :], 1, 2))
        wcat = wcat.at[:, h, 256 + h * half:256 + (h + 1) * half, MLA_KV_LORA:MLA_KV_LORA + half].set(eye)
        wcat = wcat.at[:, h, 384 + h * half:384 + (h + 1) * half,
                       MLA_KV_LORA + half:MLA_KV_LORA + 2 * half].set(eye)
    sel = jnp.zeros((256, 128), f32)
    sel = sel.at[0:half, 0:half].set(eye).at[128:128 + half, half:2 * half].set(eye)
    wuv = w["mla_w_uv"]
    wuv_x = jnp.zeros((depth, MLA_HEADS, MLA_KV_LORA, MLA_HEADS * MLA_DV), f32)
    for h in range(MLA_HEADS):
        wuv_x = wuv_x.at[:, h, :, h * MLA_DV:(h + 1) * MLA_DV].set(wuv[:, :, h, :])

    return {
        "w_in": w_all,
        "ssd_conv_w": w["ssd_conv_w"], "ssd_conv_b": row(w["ssd_conv_b"]),
        "ssd_dt_bias": row(w["ssd_dt_bias"]), "ssd_a_log": row(w["ssd_a_log"]),
        "ssd_d_x": row(jnp.repeat(w["ssd_d"], HEAD_DIM, axis=1)), "ssd_norm_g": row(w["ssd_norm_g"]),
        "gla_gate_w": w["gla_gate_w"], "gla_gate_b": row(w["gla_gate_b"]),
        "gla_norm_g_x": row(jnp.tile(w["gla_norm_g"], (1, GLA_HEADS))),
        "mla_q_norm_g_x": row(jnp.concatenate([w["mla_q_norm_g"], jnp.zeros((depth, 256 - MLA_Q_LORA), f32)], axis=1)),
        "mla_kv_norm_g": row(w["mla_kv_norm_g"]),
        "mla_w_uq_x": wuq_x, "mla_wcat": wcat.astype(bf16), "mla_sel": sel, "mla_wuv_x": wuv_x.astype(bf16),
        "w_out": w["w_out"].astype(bf16), "ln1_g": row(w["ln1_g"]), "ln1_b": row(w["ln1_b"]),
        "ffn_w_up": w["ffn_w_up"].astype(bf16), "ffn_conv_w": w["ffn_conv_w"],
        "ffn_conv_b": row(w["ffn_conv_b"]), "ffn_w_down": w["ffn_w_down"].astype(bf16),
        "ple_w_gate": w["ple_w_gate"].astype(bf16), "ple_w_proj": w["ple_w_proj"].astype(bf16),
        "ln2_g": row(w["ln2_g"]), "ln2_b": row(w["ln2_b"]),
    }


def _rope_tables(pos, rows):
    half = MLA_ROPE // 2
    inv = ROPE_THETA ** (-jnp.arange(half, dtype=f32) / half)
    ang = pos.astype(f32)[:, None] * inv
    pad = jnp.zeros((pos.shape[0], 128 - MLA_HEADS * half), f32)
    cos = jnp.concatenate([jnp.tile(jnp.cos(ang), (1, MLA_HEADS)), pad], axis=1)
    sin = jnp.concatenate([jnp.tile(jnp.sin(ang), (1, MLA_HEADS)), pad], axis=1)
    reps = max(1, rows // pos.shape[0])
    return jnp.tile(cos, (reps, 1)), jnp.tile(sin, (reps, 1))


def _halo(buf):
    lead = buf.shape[:-2]
    wdt, ch = buf.shape[-2:]
    return jnp.concatenate([jnp.zeros(lead + (8 - wdt, ch), f32), buf], axis=-2)


def _layer(x3, pe4, layer, states, st_layer, lw, cfg, cache=None):
    nseq, lp, _ = x3.shape
    c, nsq, valid, ts, ts_ffn, nsb = cfg["c"], cfg["nsq"], cfg["valid"], cfg["ts"], cfg["ts_ffn"], cfg["nsb"]
    ssm0, conv_halo0, gla0, ffn_halo0 = states
    rows = nseq * lp
    p_ssd, p_gla, p_mla = _in_proj(x3.reshape(rows, D_MODEL), lw, layer, min(rows, 1024))
    y_ssd, ssm_h, conv_halo = _ssd(p_ssd.reshape(nseq, lp, P_SSD), conv_halo0, ssm0, st_layer, lw, layer, nsq, c,
                                   valid)
    y_gla, gla_s = _gla(p_gla.reshape(nseq, lp, P_GLA), gla0, st_layer, lw, layer, nsq, c, valid)
    qcat, kcat, c_kv, k_rope = _mla_prep(p_mla.reshape(nseq, lp, P_MLA), cfg["cos"], cfg["sin"], lw, layer, nsb, ts)
    if cache is None:
        y_mla = _mla_flash(qcat, kcat, lw, layer, ts)
    else:
        cache_kv, cache_krt, page_table = cache
        y_mla = _mla_paged(qcat, kcat, cache_kv, cache_krt, page_table, lw, layer, valid)
    x_out, ffn_halo = _ffn(x3, y_ssd, y_gla, y_mla, pe4, ffn_halo0, st_layer, lw, layer, nsb, ts_ffn, valid)
    out_states = (c_kv[:, :valid], k_rope[:, :valid], ssm_h, conv_halo[:, 8 - (SSD_CONV - 1):], gla_s,
                  ffn_halo[:, 8 - (FFN_CONV - 1):])
    return x_out, out_states


def _pad_rows(a, lp):
    nseq, l = a.shape[:2]
    if l == lp:
        return a
    return jnp.concatenate([a, jnp.zeros((nseq, lp - l) + a.shape[2:], a.dtype)], axis=1)


def kernel(x_prompt, x_sample, cache_kv_latent, cache_k_rope, state_ssm, state_ssm_conv, state_gla, state_ffn_conv, page_table, p_prompt, p_sample, ln_in_g, ln_in_b, w_in, ssd_conv_w, ssd_conv_b, ssd_dt_bias, ssd_a_log, ssd_d, ssd_norm_g, gla_gate_w, gla_gate_b, gla_norm_g, mla_q_norm_g, mla_w_uq, mla_kv_norm_g, mla_w_uk, mla_w_uv, w_out, ln1_g, ln1_b, ffn_w_up, ffn_conv_w, ffn_conv_b, ffn_w_down, ple_w_proj, ple_w_gate, ln2_g, ln2_b):
    weights = dict(w_in=w_in, ssd_conv_w=ssd_conv_w, ssd_conv_b=ssd_conv_b, ssd_dt_bias=ssd_dt_bias,
                   ssd_a_log=ssd_a_log, ssd_d=ssd_d, ssd_norm_g=ssd_norm_g, gla_gate_w=gla_gate_w,
                   gla_gate_b=gla_gate_b, gla_norm_g=gla_norm_g, mla_q_norm_g=mla_q_norm_g, mla_w_uq=mla_w_uq,
                   mla_kv_norm_g=mla_kv_norm_g, mla_w_uk=mla_w_uk, mla_w_uv=mla_w_uv, w_out=w_out, ln1_g=ln1_g,
                   ln1_b=ln1_b, ffn_w_up=ffn_w_up, ffn_conv_w=ffn_conv_w, ffn_conv_b=ffn_conv_b,
                   ffn_w_down=ffn_w_down, ple_w_proj=ple_w_proj, ple_w_gate=ple_w_gate, ln2_g=ln2_g, ln2_b=ln2_b)
    depth = w_in.shape[0]
    bp, lp_, _ = x_prompt.shape
    bs, ls, _ = x_sample.shape
    t_past = page_table.shape[1] * PAGE_SIZE
    lw = _prep_weights(weights)

    g_in = ln_in_g.reshape(1, -1)
    b_in = ln_in_b.reshape(1, -1)
    hp = _ln_in(x_prompt.reshape(bp * lp_, D_MODEL), g_in, b_in, min(bp * lp_, 1024)).reshape(bp, lp_, D_MODEL)
    xs_pad = _pad_rows(x_sample, SEG_PAD)
    hs = _ln_in(xs_pad.reshape(bs * SEG_PAD, D_MODEL), g_in, b_in, bs * SEG_PAD).reshape(bs, SEG_PAD, D_MODEL)

    ts_p = min(lp_, 512)
    cos_p, sin_p = _rope_tables(jnp.arange(lp_), ts_p)
    cfg_p = dict(c=min(lp_, 128), nsq=math.gcd(bp, 4), valid=lp_, ts=ts_p, ts_ffn=ts_p, nsb=1,
                 cos=cos_p, sin=sin_p)
    cos_s, sin_s = _rope_tables(t_past + jnp.arange(SEG_PAD), bs * SEG_PAD)
    cfg_s = dict(c=SEG_PAD, nsq=math.gcd(bs, 4), valid=ls, ts=SEG_PAD, ts_ffn=SEG_PAD, nsb=bs, cos=cos_s, sin=sin_s)

    states_p = (jnp.zeros((1, bp, SSD_HEADS, HEAD_DIM, SSD_STATE), f32), jnp.zeros((1, bp, 8, SSD_CONV_CH), f32),
                jnp.zeros((1, bp, GLA_HEADS, GLA_DK, GLA_DV), f32), jnp.zeros((1, bp, 8, D_FF), f32))
    states_s = (state_ssm, _halo(state_ssm_conv), state_gla, _halo(state_ffn_conv))
    pe_s = jnp.concatenate([p_sample, jnp.zeros((depth, bs, SEG_PAD - ls, PLE_DIM), f32)], axis=2)
    cache_krt = jnp.swapaxes(cache_k_rope, 2, 3)

    prompt_states, sample_states = [], []
    for i in range(depth):
        hp, sp = _layer(hp, p_prompt, i, states_p, 0, lw, cfg_p)
        hs, ss = _layer(hs, pe_s, i, states_s, i, lw, cfg_s, cache=(cache_kv_latent, cache_krt, page_table))
        prompt_states.append(sp)
        sample_states.append(ss)
    kv_p, kr_p, ssm_p, ssm_conv_p, gla_p, ffn_conv_p = (jnp.stack(f) for f in zip(*prompt_states))
    kv_s, kr_s, ssm_s, ssm_conv_s, gla_s, ffn_conv_s = (jnp.stack(f) for f in zip(*sample_states))
    return (hp, hs[:, :ls], kv_p, kv_s, kr_p, kr_s, ssm_p, ssm_s, ssm_conv_p, ssm_conv_s, gla_p, gla_s,
            ffn_conv_p, ffn_conv_s)
```

```python
import functools
import math

import jax
import jax.numpy as jnp
import numpy as np
from jax import lax
from jax.experimental import pallas as pl
from jax.experimental.pallas import tpu as pltpu

f32 = jnp.float32
bf16 = jnp.bfloat16

D_MODEL = 1024
HEAD_DIM = 64
SSD_WIDTH = 512
SSD_HEADS = 8
SSD_GROUPS = 2
SSD_STATE = 128
SSD_CONV = 4
SSD_CONV_CH = SSD_WIDTH + 2 * SSD_GROUPS * SSD_STATE
GLA_WIDTH = 256
GLA_HEADS = 4
GLA_DK = 32
GLA_DV = 64
GLA_GATE_RANK = 16
GLA_TAU = 16.0
MLA_HEADS = 4
MLA_NOPE = 64
MLA_ROPE = 32
MLA_DV = 64
MLA_Q_LORA = 192
MLA_KV_LORA = 128
MLA_SCALE = (MLA_NOPE + MLA_ROPE) ** -0.5
EXP2_SCALE = MLA_SCALE * math.log2(math.e)
ROPE_THETA = 10000.0
D_FF = 2816
FFN_CONV = 3
PLE_DIM = 256
PAGE_SIZE = 128
DEPTH = 4
DN_ALPHA = (2 * DEPTH) ** 0.25

P_SSD = 1664
P_GLA = 896
P_MLA = 640
QK_W = 256

SEG_PAD = 8
FF_CHUNK = 256
PAGED_SLOTS = 3
PAGED_CHUNK = 2048
GLA_SAFE_LOG_DECAY = -60.0
VMEM_LIMIT = 60 * 1024 * 1024


def _bdot(a, b):
    return jnp.dot(a.astype(bf16), b.astype(bf16), preferred_element_type=f32)


def _bdot_nt(a, b):
    return lax.dot_general(a.astype(bf16), b.astype(bf16), (((1,), (1,)), ((), ())),
                           preferred_element_type=f32)


def _bdot_tn(a, b):
    return lax.dot_general(a.astype(bf16), b.astype(bf16), (((0,), (0,)), ((), ())),
                           preferred_element_type=f32)


def _split3(a):
    hi = a.astype(bf16)
    r1 = a - hi.astype(f32)
    mid = r1.astype(bf16)
    lo = (r1 - mid.astype(f32)).astype(bf16)
    return hi, mid, lo


def _dot_f32_lhs(a, m):
    m = m.astype(bf16)
    hi, mid, lo = _split3(a)
    return (jnp.dot(hi, m, preferred_element_type=f32) + jnp.dot(mid, m, preferred_element_type=f32)
            + jnp.dot(lo, m, preferred_element_type=f32))


def _dot_f32_rhs(m, a):
    m = m.astype(bf16)
    hi, mid, lo = _split3(a)
    return (jnp.dot(m, hi, preferred_element_type=f32) + jnp.dot(m, mid, preferred_element_type=f32)
            + jnp.dot(m, lo, preferred_element_type=f32))


def _dot_split2(a, b):
    a_hi = a.astype(bf16)
    a_lo = (a - a_hi.astype(f32)).astype(bf16)
    b_hi = b.astype(bf16)
    b_lo = (b - b_hi.astype(f32)).astype(bf16)
    return (jnp.dot(a_hi, b_hi, preferred_element_type=f32) + jnp.dot(a_hi, b_lo, preferred_element_type=f32)
            + jnp.dot(a_lo, b_hi, preferred_element_type=f32))


def _silu(x):
    return x * jax.nn.sigmoid(x)


def _softplus(x):
    return jnp.maximum(x, 0.0) + jnp.log1p(jnp.exp(-jnp.abs(x)))


def _log_sigmoid(x):
    return jnp.minimum(x, 0.0) - jnp.log1p(jnp.exp(-jnp.abs(x)))


def _layernorm(x, g, b, eps=1e-5):
    xc = x - jnp.mean(x, -1, keepdims=True)
    var = jnp.mean(xc * xc, -1, keepdims=True)
    return xc * lax.rsqrt(var + eps) * g + b


def _tri(c):
    r = lax.broadcasted_iota(jnp.int32, (c, c), 0)
    col = lax.broadcasted_iota(jnp.int32, (c, c), 1)
    return col <= r


def _params(sem):
    return pltpu.CompilerParams(dimension_semantics=sem, vmem_limit_bytes=VMEM_LIMIT)


def _const_spec(shape):
    nd = len(shape)
    return pl.BlockSpec(shape, lambda *_: (0,) * nd)


def _layer_spec(arr, layer):
    shape = arr.shape[1:]
    nd = len(shape)
    return pl.BlockSpec((None,) + shape, lambda *_: (layer,) + (0,) * nd, pipeline_mode=pl.Buffered(1))


def _ln_kernel(x_ref, g_ref, b_ref, o_ref):
    o_ref[...] = _layernorm(x_ref[...], g_ref[...], b_ref[...])


def _ln_in(x2, g, b, tm):
    rows = x2.shape[0]
    return pl.pallas_call(
        _ln_kernel,
        grid=(rows // tm,),
        in_specs=[pl.BlockSpec((tm, D_MODEL), lambda i: (i, 0)), _const_spec((1, D_MODEL)), _const_spec((1, D_MODEL))],
        out_specs=pl.BlockSpec((tm, D_MODEL), lambda i: (i, 0)),
        out_shape=jax.ShapeDtypeStruct((rows, D_MODEL), f32),
        compiler_params=_params(("parallel",)),
        name="ln_in",
    )(x2, g, b)


def _in_proj_kernel(x_ref, w_ref, ssd_ref, gla_ref, mla_ref):
    x = x_ref[...].astype(bf16)
    ssd_ref[...] = jnp.dot(x, w_ref[:, 0:P_SSD], preferred_element_type=f32)
    gla_ref[...] = jnp.dot(x, w_ref[:, P_SSD:P_SSD + P_GLA], preferred_element_type=f32)
    mla_ref[...] = jnp.dot(x, w_ref[:, P_SSD + P_GLA:], preferred_element_type=f32)


def _in_proj(x2, lw, layer, tm):
    rows = x2.shape[0]
    return pl.pallas_call(
        _in_proj_kernel,
        grid=(rows // tm,),
        in_specs=[pl.BlockSpec((tm, D_MODEL), lambda i: (i, 0)), _layer_spec(lw["w_in"], layer)],
        out_specs=[pl.BlockSpec((tm, P_SSD), lambda i: (i, 0)),
                   pl.BlockSpec((tm, P_GLA), lambda i: (i, 0)),
                   pl.BlockSpec((tm, P_MLA), lambda i: (i, 0))],
        out_shape=[jax.ShapeDtypeStruct((rows, P_SSD), f32),
                   jax.ShapeDtypeStruct((rows, P_GLA), f32),
                   jax.ShapeDtypeStruct((rows, P_MLA), f32)],
        compiler_params=_params(("parallel",)),
        name="in_proj",
    )(x2, lw["w_in"])


def _ssd_kernel(nsq, c, valid, p_ref, halo0_ref, h0_ref, cw_ref, cb_ref, dtb_ref, alog_ref, dexp_ref, ng_ref,
                y_ref, hout_ref, halo_out_ref, ht_scr, halo_scr):
    ci = pl.program_id(1)
    hp = SSD_HEADS * HEAD_DIM
    gw = hp // SSD_GROUPS
    heads_per_group = SSD_HEADS // SSD_GROUPS

    @pl.when(ci == 0)
    def _():
        for s in range(nsq):
            ht_scr[s] = h0_ref[s].reshape(hp, SSD_STATE).T
        halo_scr[...] = halo0_ref[...]

    row = lax.broadcasted_iota(jnp.int32, (c, 1), 0)
    tri = _tri(c)
    tri_b = tri.astype(bf16)
    expand = (lax.broadcasted_iota(jnp.int32, (SSD_HEADS, hp), 1) // HEAD_DIM
              == lax.broadcasted_iota(jnp.int32, (SSD_HEADS, hp), 0)).astype(bf16)
    lane_head = lax.broadcasted_iota(jnp.int32, (1, gw), 1) // HEAD_DIM
    cw = cw_ref[...]
    a_neg = -jnp.exp(alog_ref[...])

    for s in range(nsq):
        p = p_ref[s]
        z = p[:, 0:SSD_WIDTH]
        xr = p[:, SSD_WIDTH:SSD_WIDTH + SSD_CONV_CH]
        dt_raw = p[:, SSD_WIDTH + SSD_CONV_CH:SSD_WIDTH + SSD_CONV_CH + SSD_HEADS]

        halo = halo_scr[s]
        hm1, hm2, hm3 = halo[7:8, :], halo[6:7, :], halo[5:6, :]
        s1 = jnp.where(row >= 1, pltpu.roll(xr, 1, axis=0), hm1)
        s2 = jnp.where(row >= 2, pltpu.roll(xr, 2, axis=0), jnp.where(row == 1, hm1, hm2))
        s3 = jnp.where(row >= 3, pltpu.roll(xr, 3, axis=0),
                       jnp.where(row == 2, hm1, jnp.where(row == 1, hm2, hm3)))
        conv = cw[3:4, :] * xr + cw[2:3, :] * s1 + cw[1:2, :] * s2 + cw[0:1, :] * s3 + cb_ref[...]
        last = xr[c - 8:c, :]
        if valid < c:
            last = pltpu.roll(last, c - valid, axis=0)
        halo_scr[s] = last

        xbc = _silu(conv)
        xs = xbc[:, 0:SSD_WIDTH]
        bs = xbc[:, SSD_WIDTH:SSD_WIDTH + SSD_GROUPS * SSD_STATE]
        cs = xbc[:, SSD_WIDTH + SSD_GROUPS * SSD_STATE:]

        dt = _softplus(dt_raw + dtb_ref[...])
        if valid < c:
            dt = jnp.where(row < valid, dt, 0.0)
        la = dt * a_neg
        cum = _dot_f32_rhs(tri_b, la)
        cum_t = cum.T
        dt_t = dt.T
        w = jnp.exp(cum[c - 1:c, :] - cum) * dt
        ecum_x = _dot_f32_lhs(jnp.exp(cum), expand)
        w_x = _dot_f32_lhs(w, expand)

        ht = ht_scr[s]
        y_parts = []
        for g in range(SSD_GROUPS):
            sl = slice(g * gw, (g + 1) * gw)
            b_g = bs[:, g * SSD_STATE:(g + 1) * SSD_STATE]
            c_g = cs[:, g * SSD_STATE:(g + 1) * SSD_STATE]
            x_g = xs[:, sl]
            scores = _bdot_nt(c_g, b_g)
            y_g = _bdot(c_g, ht[:, sl]) * ecum_x[:, sl]
            for hh in range(heads_per_group):
                h = g * heads_per_group + hh
                seg = cum[:, h:h + 1] - cum_t[h:h + 1, :]
                decay = jnp.exp(jnp.where(tri, seg, -jnp.inf))
                m = scores * decay * dt_t[h:h + 1, :]
                xm = jnp.where(lane_head == hh, x_g, 0.0)
                y_g = y_g + _bdot(m, xm)
            y_parts.append(y_g)
            ht_scr[s, :, sl] = ht[:, sl] * ecum_x[c - 1:c, sl] + _bdot_tn(b_g, x_g * w_x[:, sl])
        y = jnp.concatenate(y_parts, axis=-1)
        y = (y + dexp_ref[...] * xs) * _silu(z)
        ng = ng_ref[...]
        outs = []
        for g in range(SSD_GROUPS):
            sl = slice(g * gw, (g + 1) * gw)
            yg = y[:, sl]
            outs.append(yg * lax.rsqrt(jnp.mean(yg * yg, -1, keepdims=True) + 1e-6) * ng[:, sl])
        y_ref[s] = jnp.concatenate(outs, axis=-1).astype(bf16)

    @pl.when(ci == pl.num_programs(1) - 1)
    def _():
        for s in range(nsq):
            hout_ref[s] = ht_scr[s].T.reshape(SSD_HEADS, HEAD_DIM, SSD_STATE)
        halo_out_ref[...] = halo_scr[...]


def _ssd(p_ssd, halo0, h0, st_layer, lw, layer, nsq, c, valid):
    nseq, lp, _ = p_ssd.shape
    hp = SSD_HEADS * HEAD_DIM
    return pl.pallas_call(
        functools.partial(_ssd_kernel, nsq, c, valid),
        grid=(nseq // nsq, lp // c),
        in_specs=[pl.BlockSpec((nsq, c, P_SSD), lambda b, i: (b, i, 0)),
                  pl.BlockSpec((None, nsq, 8, SSD_CONV_CH), lambda b, i: (st_layer, b, 0, 0)),
                  pl.BlockSpec((None, nsq, SSD_HEADS, HEAD_DIM, SSD_STATE), lambda b, i: (st_layer, b, 0, 0, 0)),
                  _layer_spec(lw["ssd_conv_w"], layer), _layer_spec(lw["ssd_conv_b"], layer),
                  _layer_spec(lw["ssd_dt_bias"], layer), _layer_spec(lw["ssd_a_log"], layer),
                  _layer_spec(lw["ssd_d_x"], layer), _layer_spec(lw["ssd_norm_g"], layer)],
        out_specs=[pl.BlockSpec((nsq, c, hp), lambda b, i: (b, i, 0)),
                   pl.BlockSpec((nsq, SSD_HEADS, HEAD_DIM, SSD_STATE), lambda b, i: (b, 0, 0, 0)),
                   pl.BlockSpec((nsq, 8, SSD_CONV_CH), lambda b, i: (b, 0, 0))],
        out_shape=[jax.ShapeDtypeStruct((nseq, lp, hp), bf16),
                   jax.ShapeDtypeStruct((nseq, SSD_HEADS, HEAD_DIM, SSD_STATE), f32),
                   jax.ShapeDtypeStruct((nseq, 8, SSD_CONV_CH), f32)],
        scratch_shapes=[pltpu.VMEM((nsq, SSD_STATE, hp), f32), pltpu.VMEM((nsq, 8, SSD_CONV_CH), f32)],
        compiler_params=_params(("parallel", "arbitrary")),
        name="ssd",
    )(p_ssd, halo0, h0, lw["ssd_conv_w"], lw["ssd_conv_b"], lw["ssd_dt_bias"], lw["ssd_a_log"],
      lw["ssd_d_x"], lw["ssd_norm_g"])


def _gla_kernel(nsq, c, valid, p_ref, st0_ref, gw_ref, gb_ref, gn_ref, y_ref, st_out_ref, st_scr, q_scr, b_scr,
                o_scr):
    ci = pl.program_id(1)
    hk = GLA_HEADS * GLA_DK
    hv = GLA_HEADS * GLA_DV
    blk = (lax.broadcasted_iota(jnp.int32, (hk, hv), 0) // GLA_DK
           == lax.broadcasted_iota(jnp.int32, (hk, hv), 1) // GLA_DV)

    @pl.when(ci == 0)
    def _():
        for s in range(nsq):
            s2d = st0_ref[s].reshape(hk, GLA_DV)
            st_scr[s] = jnp.where(blk, jnp.concatenate([s2d] * GLA_HEADS, axis=-1), 0.0)

    row = lax.broadcasted_iota(jnp.int32, (c, 1), 0)
    tri = _tri(c)
    tri_b = tri.astype(bf16)
    lane_k = lax.broadcasted_iota(jnp.int32, (1, hk), 1) // GLA_DK
    lane_v = lax.broadcasted_iota(jnp.int32, (1, hv), 1) // GLA_DV
    head_sum = blk.astype(bf16)
    gmat = jnp.where(lax.broadcasted_iota(jnp.int32, (hv, hv), 0) // GLA_DV
                     == lax.broadcasted_iota(jnp.int32, (hv, hv), 1) // GLA_DV, 1.0 / GLA_DV, 0.0).astype(bf16)

    seqs = []
    for s in range(nsq):
        p = p_ref[s]
        q = p[:, 0:hk] * GLA_DK ** -0.5
        k = p[:, hk:2 * hk]
        v = p[:, 2 * hk:2 * hk + hv]
        glr = p[:, 2 * hk + 2 * hv:2 * hk + 2 * hv + GLA_GATE_RANK]
        log_a = _log_sigmoid(_dot_split2(glr, gw_ref[...]) + gb_ref[...]) / GLA_TAU
        if valid < c:
            log_a = jnp.where(row < valid, log_a, 0.0)
            k = jnp.where(row < valid, k, 0.0)
        b = _dot_f32_rhs(tri_b, log_a)
        b_last = b[c - 1:c, :]
        qt = q * jnp.exp(b)
        khat = k * jnp.exp(b_last - b)
        st = st_scr[s]
        o_inter = _bdot(qt, st)
        e_col = jnp.exp(jnp.broadcast_to(b_last, (8, hk)).T[:, 0:1])
        st_scr[s] = jnp.where(blk, st * e_col + _bdot_tn(khat, v), 0.0)
        seqs.append((q, k, v, b, qt, o_inter, b_last))

    def intra_fast():
        outs = []
        for (q, k, v, b, qt, _, _) in seqs:
            kt = k * jnp.exp(-b)
            o = jnp.zeros((c, hv), f32)
            for h in range(GLA_HEADS):
                att = _bdot_nt(jnp.where(lane_k == h, qt, 0.0), kt)
                att = jnp.where(tri, att, 0.0)
                o = o + _bdot(att, jnp.where(lane_v == h, v, 0.0))
            outs.append(o)
        return outs

    def intra_exact():
        outs = []
        for (q, k, v, b, _, _, _) in seqs:
            q_scr[...] = q
            b_scr[...] = b

            def body(i, carry, k=k, v=v, b=b):
                qi = q_scr[pl.ds(i, 1), :]
                bi = b_scr[pl.ds(i, 1), :]
                t = jnp.where(row <= i, qi * k * jnp.exp(jnp.minimum(bi - b, 0.0)), 0.0)
                wgt = jnp.dot(t.astype(bf16), head_sum, preferred_element_type=f32)
                o_scr[pl.ds(i, 1), :] = jnp.sum(wgt * v, axis=0, keepdims=True)
                return carry

            lax.fori_loop(0, c, body, 0)
            outs.append(o_scr[...])
        return outs

    min_decay = seqs[0][6]
    for sq in seqs[1:]:
        min_decay = jnp.minimum(min_decay, sq[6])
    intra = lax.cond(jnp.min(min_decay) > GLA_SAFE_LOG_DECAY, intra_fast, intra_exact)

    for s in range(nsq):
        o = seqs[s][5] + intra[s]
        r = p_ref[s][:, 2 * hk + hv:2 * hk + 2 * hv]
        oo = o * o
        oo_hi = oo.astype(bf16)
        oo_lo = (oo - oo_hi.astype(f32)).astype(bf16)
        ms = jnp.dot(oo_hi, gmat, preferred_element_type=f32) + jnp.dot(oo_lo, gmat, preferred_element_type=f32)
        y = o * lax.rsqrt(ms + 1e-6) * gn_ref[...] * _silu(r)
        y_ref[s] = y.astype(bf16)

    @pl.when(ci == pl.num_programs(1) - 1)
    def _():
        for s in range(nsq):
            st = st_scr[s]
            out = st[:, 0:GLA_DV]
            for h in range(1, GLA_HEADS):
                out = out + st[:, h * GLA_DV:(h + 1) * GLA_DV]
            st_out_ref[s] = out.reshape(GLA_HEADS, GLA_DK, GLA_DV)


def _gla(p_gla, st0, st_layer, lw, layer, nsq, c, valid):
    nseq, lp, _ = p_gla.shape
    hk = GLA_HEADS * GLA_DK
    hv = GLA_HEADS * GLA_DV
    return pl.pallas_call(
        functools.partial(_gla_kernel, nsq, c, valid),
        grid=(nseq // nsq, lp // c),
        in_specs=[pl.BlockSpec((nsq, c, P_GLA), lambda b, i: (b, i, 0)),
                  pl.BlockSpec((None, nsq, GLA_HEADS, GLA_DK, GLA_DV), lambda b, i: (st_layer, b, 0, 0, 0)),
                  _layer_spec(lw["gla_gate_w"], layer), _layer_spec(lw["gla_gate_b"], layer),
                  _layer_spec(lw["gla_norm_g_x"], layer)],
        out_specs=[pl.BlockSpec((nsq, c, hv), lambda b, i: (b, i, 0)),
                   pl.BlockSpec((nsq, GLA_HEADS, GLA_DK, GLA_DV), lambda b, i: (b, 0, 0, 0))],
        out_shape=[jax.ShapeDtypeStruct((nseq, lp, hv), bf16),
                   jax.ShapeDtypeStruct((nseq, GLA_HEADS, GLA_DK, GLA_DV), f32)],
        scratch_shapes=[pltpu.VMEM((nsq, hk, hv), f32), pltpu.VMEM((c, hk), f32), pltpu.VMEM((c, hk), f32),
                        pltpu.VMEM((c, hv), f32)],
        compiler_params=_params(("parallel", "arbitrary")),
        name="gla",
    )(p_gla, st0, lw["gla_gate_w"], lw["gla_gate_b"], lw["gla_norm_g_x"])


def _mla_prep_kernel(nsb, ts, with_kt, p_ref, cos_ref, sin_ref, qg_ref, kvg_ref, wuq_ref, wcat_ref, sel_ref,
                     qcat_ref, kcat_ref, ckv_ref, krope_ref, *kt_ref):
    rows = nsb * ts
    p = p_ref[...].reshape(rows, P_MLA)
    cq = p[:, 0:256]
    ckv = p[:, 256:384]
    kra = p[:, 384:512]
    krb = p[:, 512:640]
    cos = cos_ref[...]
    sin = sin_ref[...]
    cqn = cq * lax.rsqrt(jnp.sum(cq * cq, -1, keepdims=True) * (1.0 / MLA_Q_LORA) + 1e-6) * qg_ref[...]
    qf = _bdot(cqn, wuq_ref[...])
    ra = qf[:, 256:384]
    rb = qf[:, 384:512]
    feat = jnp.concatenate([qf[:, 0:256], ra * cos - rb * sin, ra * sin + rb * cos], axis=-1).astype(bf16)
    for h in range(MLA_HEADS):
        qh = jnp.dot(feat, wcat_ref[h], preferred_element_type=f32).astype(bf16)
        qcat_ref[:, h] = qh.reshape(nsb, ts, QK_W)
    c_kv = ckv * lax.rsqrt(jnp.mean(ckv * ckv, -1, keepdims=True) + 1e-6) * kvg_ref[...]
    kr2 = jnp.concatenate([kra * cos - krb * sin, kra * sin + krb * cos], axis=-1)
    kr128 = _dot_f32_lhs(kr2, sel_ref[...])
    ckv_ref[...] = c_kv.reshape(nsb, ts, MLA_KV_LORA)
    krope_ref[...] = kr128[:, 0:MLA_ROPE].reshape(nsb, ts, MLA_ROPE)
    one_lane = (lax.broadcasted_iota(jnp.int32, (1, 128), 1) == 127).astype(f32)
    kcat = jnp.concatenate([c_kv, kr128 + one_lane], axis=-1)
    kcat_ref[...] = kcat.astype(bf16).reshape(nsb, ts, QK_W)
    if with_kt:
        kt_ref[0][0] = kcat.T.astype(bf16)


def _mla_prep(p_mla, cos, sin, lw, layer, nsb, ts, with_kt):
    nseq, lp, _ = p_mla.shape
    rows = nsb * ts
    ntile = lp // ts
    tbl_tiles = cos.shape[0] // rows
    out_specs = [pl.BlockSpec((nsb, MLA_HEADS, ts, QK_W), lambda b, i: (b, 0, i, 0)),
                 pl.BlockSpec((nsb, ts, QK_W), lambda b, i: (b, i, 0)),
                 pl.BlockSpec((nsb, ts, MLA_KV_LORA), lambda b, i: (b, i, 0)),
                 pl.BlockSpec((nsb, ts, MLA_ROPE), lambda b, i: (b, i, 0))]
    out_shape = [jax.ShapeDtypeStruct((nseq, MLA_HEADS, lp, QK_W), bf16),
                 jax.ShapeDtypeStruct((nseq, lp, QK_W), bf16),
                 jax.ShapeDtypeStruct((nseq, lp, MLA_KV_LORA), f32),
                 jax.ShapeDtypeStruct((nseq, lp, MLA_ROPE), f32)]
    if with_kt:
        assert nsb == 1
        out_specs.append(pl.BlockSpec((1, QK_W, ts), lambda b, i: (b, 0, i)))
        out_shape.append(jax.ShapeDtypeStruct((nseq, QK_W, lp), bf16))
    return pl.pallas_call(
        functools.partial(_mla_prep_kernel, nsb, ts, with_kt),
        grid=(nseq // nsb, ntile),
        in_specs=[pl.BlockSpec((nsb, ts, P_MLA), lambda b, i: (b, i, 0)),
                  pl.BlockSpec((rows, 128), lambda b, i: (i % tbl_tiles, 0)),
                  pl.BlockSpec((rows, 128), lambda b, i: (i % tbl_tiles, 0)),
                  _layer_spec(lw["mla_q_norm_g_x"], layer), _layer_spec(lw["mla_kv_norm_g"], layer),
                  _layer_spec(lw["mla_w_uq_x"], layer), _layer_spec(lw["mla_wcat"], layer),
                  _const_spec((256, 128))],
        out_specs=out_specs,
        out_shape=out_shape,
        compiler_params=_params(("parallel", "parallel")),
        name="mla_prep",
    )(p_mla, cos, sin, lw["mla_q_norm_g_x"], lw["mla_kv_norm_g"], lw["mla_w_uq_x"], lw["mla_wcat"],
      lw["mla_sel"])


def _mla_flash_kernel(tq, tk, q_ref, k_ref, kt_ref, wuvt_ref, y_ref, m_scr, acc_scr):
    qi = pl.program_id(1)
    ki = pl.program_id(2)

    @pl.when(ki == 0)
    def _():
        m_scr[...] = jnp.full(m_scr.shape, -jnp.inf, f32)
        acc_scr[...] = jnp.zeros(acc_scr.shape, f32)

    def step(masked):
        k = k_ref[0]
        kt = kt_ref[0]
        for h in range(MLA_HEADS):
            st = lax.dot_general(k, q_ref[0, h], (((1,), (1,)), ((), ())), preferred_element_type=f32)
            if masked:
                kpos = lax.broadcasted_iota(jnp.int32, (tk, tq), 0)
                qpos = lax.broadcasted_iota(jnp.int32, (tk, tq), 1)
                st = jnp.where(kpos <= qpos, st, -jnp.inf)
            m_prev = m_scr[h]
            m_new = jnp.maximum(m_prev, jnp.max(st, 0, keepdims=True))
            alpha = jnp.exp2((m_prev - m_new) * EXP2_SCALE)
            pt = jnp.exp2((st - m_new) * EXP2_SCALE)
            acc_scr[h] = alpha * acc_scr[h] + jnp.dot(kt, pt.astype(bf16), preferred_element_type=f32)
            m_scr[h] = m_new

    @pl.when(ki < qi)
    def _():
        step(False)

    @pl.when(ki == qi)
    def _():
        step(True)
        yt = jnp.zeros((MLA_HEADS * MLA_DV, tq), f32)
        for h in range(MLA_HEADS):
            acc = acc_scr[h]
            o_t = acc[0:MLA_KV_LORA, :] / acc[QK_W - 1:QK_W, :]
            yt = yt + jnp.dot(wuvt_ref[h], o_t.astype(bf16), preferred_element_type=f32)
        y_ref[0] = yt.T.astype(bf16)


def _mla_flash(qcat, kcat, kcat_t, lw, layer, tq):
    nseq, _, lp, _ = qcat.shape
    tk = tq
    nq = lp // tq
    return pl.pallas_call(
        functools.partial(_mla_flash_kernel, tq, tk),
        grid=(nseq, nq, nq),
        in_specs=[pl.BlockSpec((1, MLA_HEADS, tq, QK_W), lambda b, i, j: (b, 0, i, 0)),
                  pl.BlockSpec((1, tk, QK_W), lambda b, i, j: (b, jnp.minimum(i, j), 0)),
                  pl.BlockSpec((1, QK_W, tk), lambda b, i, j: (b, 0, jnp.minimum(i, j))),
                  _layer_spec(lw["mla_wuvt_x"], layer)],
        out_specs=pl.BlockSpec((1, tq, MLA_HEADS * MLA_DV), lambda b, i, j: (b, i, 0)),
        out_shape=jax.ShapeDtypeStruct((nseq, lp, MLA_HEADS * MLA_DV), bf16),
        scratch_shapes=[pltpu.VMEM((MLA_HEADS, 1, tq), f32), pltpu.VMEM((MLA_HEADS, QK_W, tq), f32)],
        compiler_params=_params(("parallel", "parallel", "arbitrary")),
        name="mla_flash",
    )(qcat, kcat, kcat_t, lw["mla_wuvt_x"])


def _mla_paged_kernel(valid, n_pages, layer, pt_ref, q_ref, knew_ref, wuv_ref, kv_hbm, krt_hbm, y_ref,
                      kv_buf, krt_buf, kvb_scr, s_scr, sem):
    b = pl.program_id(0)
    nb = pl.num_programs(0)
    m_rows = MLA_HEADS * SEG_PAD
    t_past = n_pages * PAGE_SIZE
    chunk = math.gcd(t_past, PAGED_CHUNK)

    def page_copies(seq, i, slot):
        pg = pt_ref[seq, i]
        rows = pl.ds(i * PAGE_SIZE, PAGE_SIZE)
        return (pltpu.make_async_copy(kv_hbm.at[layer, pg], kv_buf.at[slot, rows, :], sem.at[0, slot]),
                pltpu.make_async_copy(krt_hbm.at[layer, pg], krt_buf.at[slot, :, rows], sem.at[1, slot]))

    def start_seq(seq, slot):
        for i in range(n_pages):
            kv_cp, kr_cp = page_copies(seq, i, slot)
            kv_cp.start(priority=0)
            kr_cp.start(priority=1)

    def wait_seq(seq, slot):
        for i in range(n_pages):
            for cp in page_copies(seq, i, slot):
                cp.wait()

    @pl.when(b == 0)
    def _():
        start_seq(0, 0)
        start_seq(jnp.minimum(1, nb - 1), 1)

    slot = b % PAGED_SLOTS
    wait_seq(b, slot)
    start_seq(jnp.minimum(b + 2, nb - 1), (b + 2) % PAGED_SLOTS)

    q = q_ref[0].reshape(m_rows, QK_W)
    q_lat = q[:, 0:MLA_KV_LORA]
    q_rope = q[:, MLA_KV_LORA:MLA_KV_LORA + MLA_ROPE]
    for ci in range(t_past // chunk):
        cols = pl.ds(ci * chunk, chunk)
        kvc = kv_buf[slot, cols, :].astype(bf16)
        kvb_scr[cols, :] = kvc
        s_scr[:, cols] = (lax.dot_general(q_lat, kvc, (((1,), (1,)), ((), ())), preferred_element_type=f32)
                          + jnp.dot(q_rope, krt_buf[slot, :, cols].astype(bf16), preferred_element_type=f32))

    knew = knew_ref[0]
    sn = lax.dot_general(q, knew, (((1,), (1,)), ((), ())), preferred_element_type=f32)
    rpos = lax.broadcasted_iota(jnp.int32, (MLA_HEADS, SEG_PAD, SEG_PAD), 1).reshape(m_rows, SEG_PAD)
    cpos = lax.broadcasted_iota(jnp.int32, (m_rows, SEG_PAD), 1)
    sn = jnp.where((cpos <= rpos) & (cpos < valid), sn, -jnp.inf)
    m = jnp.maximum(jnp.max(s_scr[...], -1, keepdims=True), jnp.max(sn, -1, keepdims=True))
    pn = jnp.exp((sn - m) * MLA_SCALE)
    l = jnp.sum(pn, -1, keepdims=True)
    acc = jnp.dot(pn.astype(bf16), knew[:, 0:MLA_KV_LORA], preferred_element_type=f32)
    for ci in range(t_past // chunk):
        cols = pl.ds(ci * chunk, chunk)
        pr = jnp.exp((s_scr[:, cols] - m) * MLA_SCALE)
        l = l + jnp.sum(pr, -1, keepdims=True)
        acc = acc + jnp.dot(pr.astype(bf16), kvb_scr[cols, :], preferred_element_type=f32)
    o_lat = acc / l
    y = jnp.zeros((SEG_PAD, MLA_HEADS * MLA_DV), f32)
    for h in range(MLA_HEADS):
        y = y + _bdot(o_lat[h * SEG_PAD:(h + 1) * SEG_PAD, :], wuv_ref[h])
    y_ref[0] = y.astype(bf16)

    @pl.when(b == nb - 1)
    def _():
        wait_seq(b, (b + 1) % PAGED_SLOTS)
        wait_seq(b, (b + 2) % PAGED_SLOTS)


def _mla_paged(qcat, kcat, cache_kv, cache_krt, page_table, lw, layer, valid):
    nseq = qcat.shape[0]
    n_pages = page_table.shape[1]
    t_past = n_pages * PAGE_SIZE
    wuv = lw["mla_wuv_x"]
    grid_spec = pltpu.PrefetchScalarGridSpec(
        num_scalar_prefetch=1,
        grid=(nseq,),
        in_specs=[pl.BlockSpec((1, MLA_HEADS, SEG_PAD, QK_W), lambda b, pt: (b, 0, 0, 0)),
                  pl.BlockSpec((1, SEG_PAD, QK_W), lambda b, pt: (b, 0, 0)),
                  pl.BlockSpec((None,) + wuv.shape[1:], lambda b, pt: (layer, 0, 0, 0)),
                  pl.BlockSpec(memory_space=pl.ANY), pl.BlockSpec(memory_space=pl.ANY)],
        out_specs=pl.BlockSpec((1, SEG_PAD, MLA_HEADS * MLA_DV), lambda b, pt: (b, 0, 0)),
        scratch_shapes=[pltpu.VMEM((PAGED_SLOTS, t_past, MLA_KV_LORA), f32),
                        pltpu.VMEM((PAGED_SLOTS, MLA_ROPE, t_past), f32),
                        pltpu.VMEM((t_past, MLA_KV_LORA), bf16),
                        pltpu.VMEM((MLA_HEADS * SEG_PAD, t_past), f32),
                        pltpu.SemaphoreType.DMA((2, PAGED_SLOTS))],
    )
    return pl.pallas_call(
        functools.partial(_mla_paged_kernel, valid, n_pages, layer),
        grid_spec=grid_spec,
        out_shape=jax.ShapeDtypeStruct((nseq, SEG_PAD, MLA_HEADS * MLA_DV), bf16),
        compiler_params=_params(("arbitrary",)),
        name="mla_paged",
    )(page_table, qcat, kcat, wuv, cache_kv, cache_krt)


def _ffn_kernel(nsb, ts, valid, x_ref, ys_ref, yg_ref, ym_ref, pe_ref, halo0_ref,
                wout_ref, g1_ref, b1_ref, wup_ref, fcw_ref, fcb_ref, wdn_ref, wpg_ref, wpp_ref, g2_ref, b2_ref,
                o_ref, halo_out_ref, halo_scr, hmid_scr):
    ti = pl.program_id(1)
    rows = nsb * ts

    @pl.when(ti == 0)
    def _():
        halo_scr[...] = halo0_ref[...]

    x = x_ref[...].reshape(rows, D_MODEL)
    ycat = jnp.concatenate([ys_ref[...].reshape(rows, SSD_WIDTH), yg_ref[...].reshape(rows, GLA_WIDTH),
                            ym_ref[...].reshape(rows, MLA_HEADS * MLA_DV)], axis=-1)
    mix = jnp.dot(ycat, wout_ref[...], preferred_element_type=f32)
    x1 = _layernorm(DN_ALPHA * x + mix, g1_ref[...], b1_ref[...])
    x1b = x1.astype(bf16)

    pos = lax.broadcasted_iota(jnp.int32, (nsb, ts, 1), 1)
    for ch in range(D_FF // FF_CHUNK):
        lo = ch * FF_CHUNK
        a = jnp.dot(x1b, wup_ref[:, lo:lo + FF_CHUNK], preferred_element_type=f32)
        bgate = jnp.dot(x1b, wup_ref[:, D_FF + lo:D_FF + lo + FF_CHUNK], preferred_element_type=f32)
        a3 = a.reshape(nsb, ts, FF_CHUNK)
        halo = halo_scr[:, :, lo:lo + FF_CHUNK]
        hm1, hm2 = halo[:, 7:8, :], halo[:, 6:7, :]
        s1 = jnp.where(pos >= 1, pltpu.roll(a3, 1, axis=1), hm1)
        s2 = jnp.where(pos >= 2, pltpu.roll(a3, 2, axis=1), jnp.where(pos == 1, hm1, hm2))
        fcw = fcw_ref[:, lo:lo + FF_CHUNK]
        ac = fcw[2:3, :] * a3 + fcw[1:2, :] * s1 + fcw[0:1, :] * s2 + fcb_ref[:, lo:lo + FF_CHUNK]
        last = a3[:, ts - 8:ts, :]
        if valid < ts:
            last = pltpu.roll(last, ts - valid, axis=1)
        halo_scr[:, :, lo:lo + FF_CHUNK] = last
        ge = 0.5 * ac * (1.0 + lax.erf(ac * (1.0 / math.sqrt(2.0))))
        hmid_scr[:, lo:lo + FF_CHUNK] = (ge.reshape(rows, FF_CHUNK) * bgate).astype(bf16)
    f = jnp.dot(hmid_scr[...], wdn_ref[...], preferred_element_type=f32)

    gate = jax.nn.sigmoid(jnp.dot(x1b, wpg_ref[...], preferred_element_type=f32))
    pe = _bdot(pe_ref[...].reshape(rows, PLE_DIM), wpp_ref[...])
    x2 = _layernorm(DN_ALPHA * x1 + f + gate * pe, g2_ref[...], b2_ref[...])
    o_ref[...] = x2.reshape(nsb, ts, D_MODEL)

    @pl.when(ti == pl.num_programs(1) - 1)
    def _():
        halo_out_ref[...] = halo_scr[...]


def _ffn(x3, ys, yg, ym, pe4, halo0, st_layer, lw, layer, nsb, ts, valid):
    nseq, lp, _ = x3.shape

    def tok(width):
        return pl.BlockSpec((nsb, ts, width), lambda b, i: (b, i, 0))

    names = ["w_out", "ln1_g", "ln1_b", "ffn_w_up", "ffn_conv_w", "ffn_conv_b", "ffn_w_down", "ple_w_gate",
             "ple_w_proj", "ln2_g", "ln2_b"]
    return pl.pallas_call(
        functools.partial(_ffn_kernel, nsb, ts, valid),
        grid=(nseq // nsb, lp // ts),
        in_specs=[tok(D_MODEL), tok(SSD_WIDTH), tok(GLA_WIDTH), tok(MLA_HEADS * MLA_DV),
                  pl.BlockSpec((None, nsb, ts, PLE_DIM), lambda b, i: (layer, b, i, 0)),
                  pl.BlockSpec((None, nsb, 8, D_FF), lambda b, i: (st_layer, b, 0, 0))]
        + [_layer_spec(lw[n], layer) for n in names],
        out_specs=[tok(D_MODEL), pl.BlockSpec((nsb, 8, D_FF), lambda b, i: (b, 0, 0))],
        out_shape=[jax.ShapeDtypeStruct((nseq, lp, D_MODEL), f32),
                   jax.ShapeDtypeStruct((nseq, 8, D_FF), f32)],
        scratch_shapes=[pltpu.VMEM((nsb, 8, D_FF), f32), pltpu.VMEM((nsb * ts, D_FF), bf16)],
        compiler_params=_params(("parallel", "arbitrary")),
        name="out_ffn",
    )(x3, ys, yg, ym, pe4, halo0, *[lw[n] for n in names])


def _prep_weights(w):
    depth = w["w_in"].shape[0]

    def row(v):
        return v.reshape(depth, 1, -1).astype(f32)

    splits = np.cumsum([0, SSD_WIDTH, SSD_CONV_CH, SSD_HEADS, GLA_HEADS * GLA_DK, GLA_HEADS * GLA_DK, GLA_WIDTH,
                        GLA_WIDTH, GLA_GATE_RANK, MLA_Q_LORA, MLA_KV_LORA, MLA_ROPE]).tolist()
    w_in = w["w_in"]
    (z, xbc, dt, q, k, v, r, glr, cq, ckv, kr) = [w_in[:, :, splits[j]:splits[j + 1]] for j in range(11)]

    def zpad(n):
        return jnp.zeros((depth, D_MODEL, n), f32)

    half = MLA_ROPE // 2
    w_all = jnp.concatenate(
        [z, xbc, dt, zpad(P_SSD - SSD_WIDTH - SSD_CONV_CH - SSD_HEADS),
         q, k, v, r, glr, zpad(P_GLA - 2 * GLA_HEADS * GLA_DK - 2 * GLA_WIDTH - GLA_GATE_RANK),
         cq, zpad(256 - MLA_Q_LORA), ckv, kr[:, :, :half], zpad(128 - half), kr[:, :, half:], zpad(128 - half)],
        axis=2).astype(bf16)

    wuq = w["mla_w_uq"].reshape(depth, MLA_Q_LORA, MLA_HEADS, MLA_NOPE + MLA_ROPE)
    nope = wuq[..., :MLA_NOPE].reshape(depth, MLA_Q_LORA, MLA_HEADS * MLA_NOPE)
    ra = wuq[..., MLA_NOPE:MLA_NOPE + half].reshape(depth, MLA_Q_LORA, MLA_HEADS * half)
    rb = wuq[..., MLA_NOPE + half:].reshape(depth, MLA_Q_LORA, MLA_HEADS * half)
    zq = jnp.zeros((depth, MLA_Q_LORA, 128 - MLA_HEADS * half), f32)
    wuq_x = jnp.concatenate([nope, ra, zq, rb, zq], axis=2)
    wuq_x = jnp.concatenate([wuq_x, jnp.zeros((depth, 256 - MLA_Q_LORA, 512), f32)], axis=1).astype(bf16)

    wuk = w["mla_w_uk"]
    wcat = jnp.zeros((depth, MLA_HEADS, 512, QK_W), f32)
    eye = jnp.eye(half, dtype=f32)
    for h in range(MLA_HEADS):
        wcat = wcat.at[:, h, h * MLA_NOPE:(h + 1) * MLA_NOPE, 0:MLA_KV_LORA].set(jnp.swapaxes(wuk[:, :, h, :], 1, 2))
        wcat = wcat.at[:, h, 256 + h * half:256 + (h + 1) * half, MLA_KV_LORA:MLA_KV_LORA + half].set(eye)
        wcat = wcat.at[:, h, 384 + h * half:384 + (h + 1) * half,
                       MLA_KV_LORA + half:MLA_KV_LORA + 2 * half].set(eye)
    sel = jnp.zeros((256, 128), f32)
    sel = sel.at[0:half, 0:half].set(eye).at[128:128 + half, half:2 * half].set(eye)
    wuv = w["mla_w_uv"]
    wuv_x = jnp.zeros((depth, MLA_HEADS, MLA_KV_LORA, MLA_HEADS * MLA_DV), f32)
    for h in range(MLA_HEADS):
        wuv_x = wuv_x.at[:, h, :, h * MLA_DV:(h + 1) * MLA_DV].set(wuv[:, :, h, :])

    return {
        "w_in": w_all,
        "ssd_conv_w": w["ssd_conv_w"], "ssd_conv_b": row(w["ssd_conv_b"]),
        "ssd_dt_bias": row(w["ssd_dt_bias"]), "ssd_a_log": row(w["ssd_a_log"]),
        "ssd_d_x": row(jnp.repeat(w["ssd_d"], HEAD_DIM, axis=1)), "ssd_norm_g": row(w["ssd_norm_g"]),
        "gla_gate_w": w["gla_gate_w"], "gla_gate_b": row(w["gla_gate_b"]),
        "gla_norm_g_x": row(jnp.tile(w["gla_norm_g"], (1, GLA_HEADS))),
        "mla_q_norm_g_x": row(jnp.concatenate([w["mla_q_norm_g"], jnp.zeros((depth, 256 - MLA_Q_LORA), f32)], axis=1)),
        "mla_kv_norm_g": row(w["mla_kv_norm_g"]),
        "mla_w_uq_x": wuq_x, "mla_wcat": wcat.astype(bf16), "mla_sel": sel, "mla_wuv_x": wuv_x.astype(bf16),
        "mla_wuvt_x": jnp.swapaxes(wuv_x, 2, 3).astype(bf16),
        "w_out": w["w_out"].astype(bf16), "ln1_g": row(w["ln1_g"]), "ln1_b": row(w["ln1_b"]),
        "ffn_w_up": w["ffn_w_up"].astype(bf16), "ffn_conv_w": w["ffn_conv_w"],
        "ffn_conv_b": row(w["ffn_conv_b"]), "ffn_w_down": w["ffn_w_down"].astype(bf16),
        "ple_w_gate": w["ple_w_gate"].astype(bf16), "ple_w_proj": w["ple_w_proj"].astype(bf16),
        "ln2_g": row(w["ln2_g"]), "ln2_b": row(w["ln2_b"]),
    }


def _rope_tables(pos, rows):
    half = MLA_ROPE // 2
    inv = ROPE_THETA ** (-jnp.arange(half, dtype=f32) / half)
    ang = pos.astype(f32)[:, None] * inv
    pad = jnp.zeros((pos.shape[0], 128 - MLA_HEADS * half), f32)
    cos = jnp.concatenate([jnp.tile(jnp.cos(ang), (1, MLA_HEADS)), pad], axis=1)
    sin = jnp.concatenate([jnp.tile(jnp.sin(ang), (1, MLA_HEADS)), pad], axis=1)
    reps = max(1, rows // pos.shape[0])
    return jnp.tile(cos, (reps, 1)), jnp.tile(sin, (reps, 1))


def _halo(buf):
    lead = buf.shape[:-2]
    wdt, ch = buf.shape[-2:]
    return jnp.concatenate([jnp.zeros(lead + (8 - wdt, ch), f32), buf], axis=-2)


def _layer(x3, pe4, layer, states, st_layer, lw, cfg, cache=None):
    nseq, lp, _ = x3.shape
    c, nsq, valid, ts, ts_ffn, nsb = cfg["c"], cfg["nsq"], cfg["valid"], cfg["ts"], cfg["ts_ffn"], cfg["nsb"]
    ssm0, conv_halo0, gla0, ffn_halo0 = states
    rows = nseq * lp
    p_ssd, p_gla, p_mla = _in_proj(x3.reshape(rows, D_MODEL), lw, layer, min(rows, 1024))
    y_ssd, ssm_h, conv_halo = _ssd(p_ssd.reshape(nseq, lp, P_SSD), conv_halo0, ssm0, st_layer, lw, layer, nsq, c,
                                   valid)
    y_gla, gla_s = _gla(p_gla.reshape(nseq, lp, P_GLA), gla0, st_layer, lw, layer, nsq, c, valid)
    prep = _mla_prep(p_mla.reshape(nseq, lp, P_MLA), cfg["cos"], cfg["sin"], lw, layer, nsb, ts, cache is None)
    qcat, kcat, c_kv, k_rope = prep[:4]
    if cache is None:
        y_mla = _mla_flash(qcat, kcat, prep[4], lw, layer, ts)
    else:
        cache_kv, cache_krt, page_table = cache
        y_mla = _mla_paged(qcat, kcat, cache_kv, cache_krt, page_table, lw, layer, valid)
    x_out, ffn_halo = _ffn(x3, y_ssd, y_gla, y_mla, pe4, ffn_halo0, st_layer, lw, layer, nsb, ts_ffn, valid)
    out_states = (c_kv[:, :valid], k_rope[:, :valid], ssm_h, conv_halo[:, 8 - (SSD_CONV - 1):], gla_s,
                  ffn_halo[:, 8 - (FFN_CONV - 1):])
    return x_out, out_states


def _pad_rows(a, lp):
    nseq, l = a.shape[:2]
    if l == lp:
        return a
    return jnp.concatenate([a, jnp.zeros((nseq, lp - l) + a.shape[2:], a.dtype)], axis=1)


def kernel(x_prompt, x_sample, cache_kv_latent, cache_k_rope, state_ssm, state_ssm_conv, state_gla, state_ffn_conv, page_table, p_prompt, p_sample, ln_in_g, ln_in_b, w_in, ssd_conv_w, ssd_conv_b, ssd_dt_bias, ssd_a_log, ssd_d, ssd_norm_g, gla_gate_w, gla_gate_b, gla_norm_g, mla_q_norm_g, mla_w_uq, mla_kv_norm_g, mla_w_uk, mla_w_uv, w_out, ln1_g, ln1_b, ffn_w_up, ffn_conv_w, ffn_conv_b, ffn_w_down, ple_w_proj, ple_w_gate, ln2_g, ln2_b):
    weights = dict(w_in=w_in, ssd_conv_w=ssd_conv_w, ssd_conv_b=ssd_conv_b, ssd_dt_bias=ssd_dt_bias,
                   ssd_a_log=ssd_a_log, ssd_d=ssd_d, ssd_norm_g=ssd_norm_g, gla_gate_w=gla_gate_w,
                   gla_gate_b=gla_gate_b, gla_norm_g=gla_norm_g, mla_q_norm_g=mla_q_norm_g, mla_w_uq=mla_w_uq,
                   mla_kv_norm_g=mla_kv_norm_g, mla_w_uk=mla_w_uk, mla_w_uv=mla_w_uv, w_out=w_out, ln1_g=ln1_g,
                   ln1_b=ln1_b, ffn_w_up=ffn_w_up, ffn_conv_w=ffn_conv_w, ffn_conv_b=ffn_conv_b,
                   ffn_w_down=ffn_w_down, ple_w_proj=ple_w_proj, ple_w_gate=ple_w_gate, ln2_g=ln2_g, ln2_b=ln2_b)
    depth = w_in.shape[0]
    bp, lp_, _ = x_prompt.shape
    bs, ls, _ = x_sample.shape
    t_past = page_table.shape[1] * PAGE_SIZE
    lw = _prep_weights(weights)

    g_in = ln_in_g.reshape(1, -1)
    b_in = ln_in_b.reshape(1, -1)
    hp = _ln_in(x_prompt.reshape(bp * lp_, D_MODEL), g_in, b_in, min(bp * lp_, 1024)).reshape(bp, lp_, D_MODEL)
    xs_pad = _pad_rows(x_sample, SEG_PAD)
    hs = _ln_in(xs_pad.reshape(bs * SEG_PAD, D_MODEL), g_in, b_in, bs * SEG_PAD).reshape(bs, SEG_PAD, D_MODEL)

    ts_p = min(lp_, 512)
    cos_p, sin_p = _rope_tables(jnp.arange(lp_), ts_p)
    cfg_p = dict(c=min(lp_, 128), nsq=math.gcd(bp, 4), valid=lp_, ts=ts_p, ts_ffn=ts_p, nsb=1,
                 cos=cos_p, sin=sin_p)
    cos_s, sin_s = _rope_tables(t_past + jnp.arange(SEG_PAD), bs * SEG_PAD)
    cfg_s = dict(c=SEG_PAD, nsq=math.gcd(bs, 8), valid=ls, ts=SEG_PAD, ts_ffn=SEG_PAD, nsb=bs, cos=cos_s, sin=sin_s)

    states_p = (jnp.zeros((1, bp, SSD_HEADS, HEAD_DIM, SSD_STATE), f32), jnp.zeros((1, bp, 8, SSD_CONV_CH), f32),
                jnp.zeros((1, bp, GLA_HEADS, GLA_DK, GLA_DV), f32), jnp.zeros((1, bp, 8, D_FF), f32))
    states_s = (state_ssm, _halo(state_ssm_conv), state_gla, _halo(state_ffn_conv))
    pe_s = jnp.concatenate([p_sample, jnp.zeros((depth, bs, SEG_PAD - ls, PLE_DIM), f32)], axis=2)
    cache_krt = jnp.swapaxes(cache_k_rope, 2, 3)

    prompt_states, sample_states = [], []
    for i in range(depth):
        hp, sp = _layer(hp, p_prompt, i, states_p, 0, lw, cfg_p)
        hs, ss = _layer(hs, pe_s, i, states_s, i, lw, cfg_s, cache=(cache_kv_latent, cache_krt, page_table))
        prompt_states.append(sp)
        sample_states.append(ss)
    kv_p, kr_p, ssm_p, ssm_conv_p, gla_p, ffn_conv_p = (jnp.stack(f) for f in zip(*prompt_states))
    kv_s, kr_s, ssm_s, ssm_conv_s, gla_s, ffn_conv_s = (jnp.stack(f) for f in zip(*sample_states))
    return (hp, hs[:, :ls], kv_p, kv_s, kr_p, kr_s, ssm_p, ssm_s, ssm_conv_p, ssm_conv_s, gla_p, gla_s,
            ffn_conv_p, ffn_conv_s)
```

```python
import functools
import math

import jax
import jax.numpy as jnp
import numpy as np
from jax import lax
from jax.experimental import pallas as pl
from jax.experimental.pallas import tpu as pltpu

f32 = jnp.float32
bf16 = jnp.bfloat16

D_MODEL = 1024
HEAD_DIM = 64
SSD_WIDTH = 512
SSD_HEADS = 8
SSD_GROUPS = 2
SSD_STATE = 128
SSD_CONV = 4
SSD_CONV_CH = SSD_WIDTH + 2 * SSD_GROUPS * SSD_STATE
GLA_WIDTH = 256
GLA_HEADS = 4
GLA_DK = 32
GLA_DV = 64
GLA_GATE_RANK = 16
GLA_TAU = 16.0
MLA_HEADS = 4
MLA_NOPE = 64
MLA_ROPE = 32
MLA_DV = 64
MLA_Q_LORA = 192
MLA_KV_LORA = 128
MLA_SCALE = (MLA_NOPE + MLA_ROPE) ** -0.5
EXP2_SCALE = MLA_SCALE * math.log2(math.e)
ROPE_THETA = 10000.0
D_FF = 2816
FFN_CONV = 3
PLE_DIM = 256
PAGE_SIZE = 128
DEPTH = 4
DN_ALPHA = (2 * DEPTH) ** 0.25

P_SSD = 1664
P_GLA = 896
P_MLA = 640
QK_W = 256

SEG_PAD = 8
FF_CHUNK = 256
PAGED_SLOTS = 3
PAGED_CHUNK = 2048
GLA_SAFE_LOG_DECAY = -60.0
VMEM_LIMIT = 60 * 1024 * 1024


def _bdot(a, b):
    return jnp.dot(a.astype(bf16), b.astype(bf16), preferred_element_type=f32)


def _bdot_nt(a, b):
    return lax.dot_general(a.astype(bf16), b.astype(bf16), (((1,), (1,)), ((), ())),
                           preferred_element_type=f32)


def _bdot_tn(a, b):
    return lax.dot_general(a.astype(bf16), b.astype(bf16), (((0,), (0,)), ((), ())),
                           preferred_element_type=f32)


def _split3(a):
    hi = a.astype(bf16)
    r1 = a - hi.astype(f32)
    mid = r1.astype(bf16)
    lo = (r1 - mid.astype(f32)).astype(bf16)
    return hi, mid, lo


def _dot_f32_lhs(a, m):
    m = m.astype(bf16)
    hi, mid, lo = _split3(a)
    return (jnp.dot(hi, m, preferred_element_type=f32) + jnp.dot(mid, m, preferred_element_type=f32)
            + jnp.dot(lo, m, preferred_element_type=f32))


def _dot_f32_rhs(m, a):
    m = m.astype(bf16)
    hi, mid, lo = _split3(a)
    return (jnp.dot(m, hi, preferred_element_type=f32) + jnp.dot(m, mid, preferred_element_type=f32)
            + jnp.dot(m, lo, preferred_element_type=f32))


def _dot_split2(a, b):
    a_hi = a.astype(bf16)
    a_lo = (a - a_hi.astype(f32)).astype(bf16)
    b_hi = b.astype(bf16)
    b_lo = (b - b_hi.astype(f32)).astype(bf16)
    return (jnp.dot(a_hi, b_hi, preferred_element_type=f32) + jnp.dot(a_hi, b_lo, preferred_element_type=f32)
            + jnp.dot(a_lo, b_hi, preferred_element_type=f32))


def _silu(x):
    return x * jax.nn.sigmoid(x)


def _softplus(x):
    return jnp.maximum(x, 0.0) + jnp.log1p(jnp.exp(-jnp.abs(x)))


def _log_sigmoid(x):
    return jnp.minimum(x, 0.0) - jnp.log1p(jnp.exp(-jnp.abs(x)))


def _layernorm(x, g, b, eps=1e-5):
    xc = x - jnp.mean(x, -1, keepdims=True)
    var = jnp.mean(xc * xc, -1, keepdims=True)
    return xc * lax.rsqrt(var + eps) * g + b


def _tri(c):
    r = lax.broadcasted_iota(jnp.int32, (c, c), 0)
    col = lax.broadcasted_iota(jnp.int32, (c, c), 1)
    return col <= r


def _params(sem):
    return pltpu.CompilerParams(dimension_semantics=sem, vmem_limit_bytes=VMEM_LIMIT)


def _const_spec(shape):
    nd = len(shape)
    return pl.BlockSpec(shape, lambda *_: (0,) * nd)


def _layer_spec(arr, layer):
    shape = arr.shape[1:]
    nd = len(shape)
    return pl.BlockSpec((None,) + shape, lambda *_: (layer,) + (0,) * nd, pipeline_mode=pl.Buffered(1))


def _ln_kernel(x_ref, g_ref, b_ref, o_ref):
    o_ref[...] = _layernorm(x_ref[...], g_ref[...], b_ref[...])


def _ln_in(x2, g, b, tm):
    rows = x2.shape[0]
    return pl.pallas_call(
        _ln_kernel,
        grid=(rows // tm,),
        in_specs=[pl.BlockSpec((tm, D_MODEL), lambda i: (i, 0)), _const_spec((1, D_MODEL)), _const_spec((1, D_MODEL))],
        out_specs=pl.BlockSpec((tm, D_MODEL), lambda i: (i, 0)),
        out_shape=jax.ShapeDtypeStruct((rows, D_MODEL), f32),
        compiler_params=_params(("parallel",)),
        name="ln_in",
    )(x2, g, b)


def _in_proj_kernel(x_ref, w_ref, ssd_ref, gla_ref, mla_ref):
    x = x_ref[...].astype(bf16)
    ssd_ref[...] = jnp.dot(x, w_ref[:, 0:P_SSD], preferred_element_type=f32)
    gla_ref[...] = jnp.dot(x, w_ref[:, P_SSD:P_SSD + P_GLA], preferred_element_type=f32)
    mla_ref[...] = jnp.dot(x, w_ref[:, P_SSD + P_GLA:], preferred_element_type=f32)


def _in_proj(x2, lw, layer, tm):
    rows = x2.shape[0]
    return pl.pallas_call(
        _in_proj_kernel,
        grid=(rows // tm,),
        in_specs=[pl.BlockSpec((tm, D_MODEL), lambda i: (i, 0)), _layer_spec(lw["w_in"], layer)],
        out_specs=[pl.BlockSpec((tm, P_SSD), lambda i: (i, 0)),
                   pl.BlockSpec((tm, P_GLA), lambda i: (i, 0)),
                   pl.BlockSpec((tm, P_MLA), lambda i: (i, 0))],
        out_shape=[jax.ShapeDtypeStruct((rows, P_SSD), f32),
                   jax.ShapeDtypeStruct((rows, P_GLA), f32),
                   jax.ShapeDtypeStruct((rows, P_MLA), f32)],
        compiler_params=_params(("parallel",)),
        name="in_proj",
    )(x2, lw["w_in"])


def _ssd_kernel(nsq, c, valid, p_ref, halo0_ref, h0_ref, cw_ref, cb_ref, dtb_ref, alog_ref, dexp_ref, ng_ref,
                y_ref, hout_ref, halo_out_ref, ht_scr, halo_scr):
    ci = pl.program_id(1)
    hp = SSD_HEADS * HEAD_DIM
    gw = hp // SSD_GROUPS
    heads_per_group = SSD_HEADS // SSD_GROUPS

    @pl.when(ci == 0)
    def _():
        for s in range(nsq):
            ht_scr[s] = h0_ref[s].reshape(hp, SSD_STATE).T
        halo_scr[...] = halo0_ref[...]

    row = lax.broadcasted_iota(jnp.int32, (c, 1), 0)
    tri = _tri(c)
    tri_b = tri.astype(bf16)
    expand = (lax.broadcasted_iota(jnp.int32, (SSD_HEADS, hp), 1) // HEAD_DIM
              == lax.broadcasted_iota(jnp.int32, (SSD_HEADS, hp), 0)).astype(bf16)
    lane_head = lax.broadcasted_iota(jnp.int32, (1, gw), 1) // HEAD_DIM
    cw = cw_ref[...]
    a_neg = -jnp.exp(alog_ref[...])

    sq = range(nsq)
    groups = range(SSD_GROUPS)
    gsl = [slice(g * gw, (g + 1) * gw) for g in groups]
    zs, xss, bss, css, dts, las = [], [], [], [], [], []
    for s in sq:
        p = p_ref[s]
        xr = p[:, SSD_WIDTH:SSD_WIDTH + SSD_CONV_CH]
        dt_raw = p[:, SSD_WIDTH + SSD_CONV_CH:SSD_WIDTH + SSD_CONV_CH + SSD_HEADS]
        halo = halo_scr[s]
        hm1, hm2, hm3 = halo[7:8, :], halo[6:7, :], halo[5:6, :]
        s1 = jnp.where(row >= 1, pltpu.roll(xr, 1, axis=0), hm1)
        s2 = jnp.where(row >= 2, pltpu.roll(xr, 2, axis=0), jnp.where(row == 1, hm1, hm2))
        s3 = jnp.where(row >= 3, pltpu.roll(xr, 3, axis=0),
                       jnp.where(row == 2, hm1, jnp.where(row == 1, hm2, hm3)))
        conv = cw[3:4, :] * xr + cw[2:3, :] * s1 + cw[1:2, :] * s2 + cw[0:1, :] * s3 + cb_ref[...]
        last = xr[c - 8:c, :]
        if valid < c:
            last = pltpu.roll(last, c - valid, axis=0)
        halo_scr[s] = last
        xbc = _silu(conv)
        dt = _softplus(dt_raw + dtb_ref[...])
        if valid < c:
            dt = jnp.where(row < valid, dt, 0.0)
        zs.append(p[:, 0:SSD_WIDTH])
        xss.append(xbc[:, 0:SSD_WIDTH])
        bss.append(xbc[:, SSD_WIDTH:SSD_WIDTH + SSD_GROUPS * SSD_STATE])
        css.append(xbc[:, SSD_WIDTH + SSD_GROUPS * SSD_STATE:])
        dts.append(dt)
        las.append(dt * a_neg)

    cums = [_dot_f32_rhs(tri_b, las[s]) for s in sq]
    ecum_xs = [_dot_f32_lhs(jnp.exp(cums[s]), expand) for s in sq]
    w_xs = [_dot_f32_lhs(jnp.exp(cums[s][c - 1:c, :] - cums[s]) * dts[s], expand) for s in sq]
    hts = [ht_scr[s] for s in sq]
    b_g = [[bss[s][:, g * SSD_STATE:(g + 1) * SSD_STATE] for g in groups] for s in sq]
    c_g = [[css[s][:, g * SSD_STATE:(g + 1) * SSD_STATE] for g in groups] for s in sq]
    scores = [[_bdot_nt(c_g[s][g], b_g[s][g]) for g in groups] for s in sq]
    y_state = [[_bdot(c_g[s][g], hts[s][:, gsl[g]]) for g in groups] for s in sq]
    upd = [[_bdot_tn(b_g[s][g], xss[s][:, gsl[g]] * w_xs[s][:, gsl[g]]) for g in groups] for s in sq]
    for s in sq:
        for g in groups:
            ht_scr[s, :, gsl[g]] = hts[s][:, gsl[g]] * ecum_xs[s][c - 1:c, gsl[g]] + upd[s][g]

    ms = []
    for s in sq:
        cum_t = cums[s].T
        dt_t = dts[s].T
        per_head = []
        for h in range(SSD_HEADS):
            seg = cums[s][:, h:h + 1] - cum_t[h:h + 1, :]
            decay = jnp.exp(jnp.where(tri, seg, -jnp.inf))
            per_head.append((scores[s][h // heads_per_group] * decay * dt_t[h:h + 1, :]).astype(bf16))
        ms.append(per_head)
    ys = []
    for s in sq:
        y_parts = []
        for g in groups:
            x_g = xss[s][:, gsl[g]]
            y_g = y_state[s][g] * ecum_xs[s][:, gsl[g]]
            for hh in range(heads_per_group):
                xm = jnp.where(lane_head == hh, x_g, 0.0).astype(bf16)
                y_g = y_g + jnp.dot(ms[s][g * heads_per_group + hh], xm, preferred_element_type=f32)
            y_parts.append(y_g)
        ys.append(jnp.concatenate(y_parts, axis=-1))
    ng = ng_ref[...]
    for s in sq:
        y = (ys[s] + dexp_ref[...] * xss[s]) * _silu(zs[s])
        outs = []
        for g in groups:
            yg = y[:, gsl[g]]
            outs.append(yg * lax.rsqrt(jnp.mean(yg * yg, -1, keepdims=True) + 1e-6) * ng[:, gsl[g]])
        y_ref[s] = jnp.concatenate(outs, axis=-1).astype(bf16)

    @pl.when(ci == pl.num_programs(1) - 1)
    def _():
        for s in range(nsq):
            hout_ref[s] = ht_scr[s].T.reshape(SSD_HEADS, HEAD_DIM, SSD_STATE)
        halo_out_ref[...] = halo_scr[...]


def _ssd(p_ssd, halo0, h0, st_layer, lw, layer, nsq, c, valid):
    nseq, lp, _ = p_ssd.shape
    hp = SSD_HEADS * HEAD_DIM
    return pl.pallas_call(
        functools.partial(_ssd_kernel, nsq, c, valid),
        grid=(nseq // nsq, lp // c),
        in_specs=[pl.BlockSpec((nsq, c, P_SSD), lambda b, i: (b, i, 0)),
                  pl.BlockSpec((None, nsq, 8, SSD_CONV_CH), lambda b, i: (st_layer, b, 0, 0)),
                  pl.BlockSpec((None, nsq, SSD_HEADS, HEAD_DIM, SSD_STATE), lambda b, i: (st_layer, b, 0, 0, 0)),
                  _layer_spec(lw["ssd_conv_w"], layer), _layer_spec(lw["ssd_conv_b"], layer),
                  _layer_spec(lw["ssd_dt_bias"], layer), _layer_spec(lw["ssd_a_log"], layer),
                  _layer_spec(lw["ssd_d_x"], layer), _layer_spec(lw["ssd_norm_g"], layer)],
        out_specs=[pl.BlockSpec((nsq, c, hp), lambda b, i: (b, i, 0)),
                   pl.BlockSpec((nsq, SSD_HEADS, HEAD_DIM, SSD_STATE), lambda b, i: (b, 0, 0, 0)),
                   pl.BlockSpec((nsq, 8, SSD_CONV_CH), lambda b, i: (b, 0, 0))],
        out_shape=[jax.ShapeDtypeStruct((nseq, lp, hp), bf16),
                   jax.ShapeDtypeStruct((nseq, SSD_HEADS, HEAD_DIM, SSD_STATE), f32),
                   jax.ShapeDtypeStruct((nseq, 8, SSD_CONV_CH), f32)],
        scratch_shapes=[pltpu.VMEM((nsq, SSD_STATE, hp), f32), pltpu.VMEM((nsq, 8, SSD_CONV_CH), f32)],
        compiler_params=_params(("parallel", "arbitrary")),
        name="ssd",
    )(p_ssd, halo0, h0, lw["ssd_conv_w"], lw["ssd_conv_b"], lw["ssd_dt_bias"], lw["ssd_a_log"],
      lw["ssd_d_x"], lw["ssd_norm_g"])


def _gla_kernel(nsq, c, valid, p_ref, st0_ref, gw_ref, gb_ref, gn_ref, y_ref, st_out_ref, st_scr, q_scr, b_scr,
                o_scr):
    ci = pl.program_id(1)
    hk = GLA_HEADS * GLA_DK
    hv = GLA_HEADS * GLA_DV
    blk = (lax.broadcasted_iota(jnp.int32, (hk, hv), 0) // GLA_DK
           == lax.broadcasted_iota(jnp.int32, (hk, hv), 1) // GLA_DV)

    @pl.when(ci == 0)
    def _():
        for s in range(nsq):
            s2d = st0_ref[s].reshape(hk, GLA_DV)
            st_scr[s] = jnp.where(blk, jnp.concatenate([s2d] * GLA_HEADS, axis=-1), 0.0)

    row = lax.broadcasted_iota(jnp.int32, (c, 1), 0)
    tri = _tri(c)
    tri_b = tri.astype(bf16)
    lane_k = lax.broadcasted_iota(jnp.int32, (1, hk), 1) // GLA_DK
    lane_v = lax.broadcasted_iota(jnp.int32, (1, hv), 1) // GLA_DV
    head_sum = blk.astype(bf16)
    gmat = jnp.where(lax.broadcasted_iota(jnp.int32, (hv, hv), 0) // GLA_DV
                     == lax.broadcasted_iota(jnp.int32, (hv, hv), 1) // GLA_DV, 1.0 / GLA_DV, 0.0).astype(bf16)

    sq = range(nsq)
    ps = [p_ref[s] for s in sq]
    gate = [_dot_split2(ps[s][:, 2 * hk + 2 * hv:2 * hk + 2 * hv + GLA_GATE_RANK], gw_ref[...]) for s in sq]
    ks = [ps[s][:, hk:2 * hk] for s in sq]
    vs = [ps[s][:, 2 * hk:2 * hk + hv] for s in sq]
    log_as = [_log_sigmoid(gate[s] + gb_ref[...]) / GLA_TAU for s in sq]
    if valid < c:
        log_as = [jnp.where(row < valid, la, 0.0) for la in log_as]
        ks = [jnp.where(row < valid, k, 0.0) for k in ks]
    bs = [_dot_f32_rhs(tri_b, la) for la in log_as]
    b_lasts = [b[c - 1:c, :] for b in bs]
    qs = [ps[s][:, 0:hk] * GLA_DK ** -0.5 for s in sq]
    qts = [qs[s] * jnp.exp(bs[s]) for s in sq]
    sts = [st_scr[s] for s in sq]
    o_inters = [_bdot(qts[s], sts[s]) for s in sq]
    upd = [_bdot_tn(ks[s] * jnp.exp(b_lasts[s] - bs[s]), vs[s]) for s in sq]
    for s in sq:
        e_col = jnp.exp(jnp.broadcast_to(b_lasts[s], (8, hk)).T[:, 0:1])
        st_scr[s] = jnp.where(blk, sts[s] * e_col + upd[s], 0.0)
    seqs = [(qs[s], ks[s], vs[s], bs[s], qts[s], o_inters[s], b_lasts[s]) for s in sq]

    def intra_fast():
        kts = [ks[s] * jnp.exp(-bs[s]) for s in sq]
        atts = [[_bdot_nt(jnp.where(lane_k == h, qts[s], 0.0), kts[s]) for h in range(GLA_HEADS)] for s in sq]
        outs = []
        for s in sq:
            o = jnp.zeros((c, hv), f32)
            for h in range(GLA_HEADS):
                o = o + _bdot(jnp.where(tri, atts[s][h], 0.0), jnp.where(lane_v == h, vs[s], 0.0))
            outs.append(o)
        return outs

    def intra_exact():
        outs = []
        for (q, k, v, b, _, _, _) in seqs:
            q_scr[...] = q
            b_scr[...] = b

            def body(i, carry, k=k, v=v, b=b):
                qi = q_scr[pl.ds(i, 1), :]
                bi = b_scr[pl.ds(i, 1), :]
                t = jnp.where(row <= i, qi * k * jnp.exp(jnp.minimum(bi - b, 0.0)), 0.0)
                wgt = jnp.dot(t.astype(bf16), head_sum, preferred_element_type=f32)
                o_scr[pl.ds(i, 1), :] = jnp.sum(wgt * v, axis=0, keepdims=True)
                return carry

            lax.fori_loop(0, c, body, 0)
            outs.append(o_scr[...])
        return outs

    min_decay = b_lasts[0]
    for bl in b_lasts[1:]:
        min_decay = jnp.minimum(min_decay, bl)
    intra = lax.cond(jnp.min(min_decay) > GLA_SAFE_LOG_DECAY, intra_fast, intra_exact)

    os_ = [o_inters[s] + intra[s] for s in sq]
    mss = []
    for s in sq:
        oo = os_[s] * os_[s]
        oo_hi = oo.astype(bf16)
        oo_lo = (oo - oo_hi.astype(f32)).astype(bf16)
        mss.append(jnp.dot(oo_hi, gmat, preferred_element_type=f32)
                   + jnp.dot(oo_lo, gmat, preferred_element_type=f32))
    for s in sq:
        r = ps[s][:, 2 * hk + hv:2 * hk + 2 * hv]
        y_ref[s] = (os_[s] * lax.rsqrt(mss[s] + 1e-6) * gn_ref[...] * _silu(r)).astype(bf16)

    @pl.when(ci == pl.num_programs(1) - 1)
    def _():
        for s in range(nsq):
            st = st_scr[s]
            out = st[:, 0:GLA_DV]
            for h in range(1, GLA_HEADS):
                out = out + st[:, h * GLA_DV:(h + 1) * GLA_DV]
            st_out_ref[s] = out.reshape(GLA_HEADS, GLA_DK, GLA_DV)


def _gla(p_gla, st0, st_layer, lw, layer, nsq, c, valid):
    nseq, lp, _ = p_gla.shape
    hk = GLA_HEADS * GLA_DK
    hv = GLA_HEADS * GLA_DV
    return pl.pallas_call(
        functools.partial(_gla_kernel, nsq, c, valid),
        grid=(nseq // nsq, lp // c),
        in_specs=[pl.BlockSpec((nsq, c, P_GLA), lambda b, i: (b, i, 0)),
                  pl.BlockSpec((None, nsq, GLA_HEADS, GLA_DK, GLA_DV), lambda b, i: (st_layer, b, 0, 0, 0)),
                  _layer_spec(lw["gla_gate_w"], layer), _layer_spec(lw["gla_gate_b"], layer),
                  _layer_spec(lw["gla_norm_g_x"], layer)],
        out_specs=[pl.BlockSpec((nsq, c, hv), lambda b, i: (b, i, 0)),
                   pl.BlockSpec((nsq, GLA_HEADS, GLA_DK, GLA_DV), lambda b, i: (b, 0, 0, 0))],
        out_shape=[jax.ShapeDtypeStruct((nseq, lp, hv), bf16),
                   jax.ShapeDtypeStruct((nseq, GLA_HEADS, GLA_DK, GLA_DV), f32)],
        scratch_shapes=[pltpu.VMEM((nsq, hk, hv), f32), pltpu.VMEM((c, hk), f32), pltpu.VMEM((c, hk), f32),
                        pltpu.VMEM((c, hv), f32)],
        compiler_params=_params(("parallel", "arbitrary")),
        name="gla",
    )(p_gla, st0, lw["gla_gate_w"], lw["gla_gate_b"], lw["gla_norm_g_x"])


def _mla_prep_kernel(nsb, ts, with_kt, p_ref, cos_ref, sin_ref, qg_ref, kvg_ref, wuq_ref, wcat_ref, sel_ref,
                     qcat_ref, kcat_ref, ckv_ref, krope_ref, *kt_ref):
    rows = nsb * ts
    p = p_ref[...].reshape(rows, P_MLA)
    cq = p[:, 0:256]
    ckv = p[:, 256:384]
    kra = p[:, 384:512]
    krb = p[:, 512:640]
    cos = cos_ref[...]
    sin = sin_ref[...]
    cqn = cq * lax.rsqrt(jnp.sum(cq * cq, -1, keepdims=True) * (1.0 / MLA_Q_LORA) + 1e-6) * qg_ref[...]
    qf = _bdot(cqn, wuq_ref[...])
    ra = qf[:, 256:384]
    rb = qf[:, 384:512]
    feat = jnp.concatenate([qf[:, 0:256], ra * cos - rb * sin, ra * sin + rb * cos], axis=-1).astype(bf16)
    for h in range(MLA_HEADS):
        qh = jnp.dot(feat, wcat_ref[h], preferred_element_type=f32).astype(bf16)
        qcat_ref[:, h] = qh.reshape(nsb, ts, QK_W)
    c_kv = ckv * lax.rsqrt(jnp.mean(ckv * ckv, -1, keepdims=True) + 1e-6) * kvg_ref[...]
    kr2 = jnp.concatenate([kra * cos - krb * sin, kra * sin + krb * cos], axis=-1)
    kr128 = _dot_f32_lhs(kr2, sel_ref[...])
    ckv_ref[...] = c_kv.reshape(nsb, ts, MLA_KV_LORA)
    krope_ref[...] = kr128[:, 0:MLA_ROPE].reshape(nsb, ts, MLA_ROPE)
    one_lane = (lax.broadcasted_iota(jnp.int32, (1, 128), 1) == 127).astype(f32)
    kcat = jnp.concatenate([c_kv, kr128 + one_lane], axis=-1)
    kcat_ref[...] = kcat.astype(bf16).reshape(nsb, ts, QK_W)
    if with_kt:
        kt_ref[0][0] = kcat.T.astype(bf16)


def _mla_prep(p_mla, cos, sin, lw, layer, nsb, ts, with_kt):
    nseq, lp, _ = p_mla.shape
    rows = nsb * ts
    ntile = lp // ts
    tbl_tiles = cos.shape[0] // rows
    out_specs = [pl.BlockSpec((nsb, MLA_HEADS, ts, QK_W), lambda b, i: (b, 0, i, 0)),
                 pl.BlockSpec((nsb, ts, QK_W), lambda b, i: (b, i, 0)),
                 pl.BlockSpec((nsb, ts, MLA_KV_LORA), lambda b, i: (b, i, 0)),
                 pl.BlockSpec((nsb, ts, MLA_ROPE), lambda b, i: (b, i, 0))]
    out_shape = [jax.ShapeDtypeStruct((nseq, MLA_HEADS, lp, QK_W), bf16),
                 jax.ShapeDtypeStruct((nseq, lp, QK_W), bf16),
                 jax.ShapeDtypeStruct((nseq, lp, MLA_KV_LORA), f32),
                 jax.ShapeDtypeStruct((nseq, lp, MLA_ROPE), f32)]
    if with_kt:
        assert nsb == 1
        out_specs.append(pl.BlockSpec((1, QK_W, ts), lambda b, i: (b, 0, i)))
        out_shape.append(jax.ShapeDtypeStruct((nseq, QK_W, lp), bf16))
    return pl.pallas_call(
        functools.partial(_mla_prep_kernel, nsb, ts, with_kt),
        grid=(nseq // nsb, ntile),
        in_specs=[pl.BlockSpec((nsb, ts, P_MLA), lambda b, i: (b, i, 0)),
                  pl.BlockSpec((rows, 128), lambda b, i: (i % tbl_tiles, 0)),
                  pl.BlockSpec((rows, 128), lambda b, i: (i % tbl_tiles, 0)),
                  _layer_spec(lw["mla_q_norm_g_x"], layer), _layer_spec(lw["mla_kv_norm_g"], layer),
                  _layer_spec(lw["mla_w_uq_x"], layer), _layer_spec(lw["mla_wcat"], layer),
                  _const_spec((256, 128))],
        out_specs=out_specs,
        out_shape=out_shape,
        compiler_params=_params(("parallel", "parallel")),
        name="mla_prep",
    )(p_mla, cos, sin, lw["mla_q_norm_g_x"], lw["mla_kv_norm_g"], lw["mla_w_uq_x"], lw["mla_wcat"],
      lw["mla_sel"])


def _mla_flash_kernel(tq, tk, qi_ref, ki_ref, q_ref, k_ref, kt_ref, wuvt_ref, y_ref, m_scr, acc_scr):
    qi = qi_ref[pl.program_id(1)]
    ki = ki_ref[pl.program_id(1)]

    @pl.when(ki == 0)
    def _():
        m_scr[...] = jnp.full(m_scr.shape, -jnp.inf, f32)
        acc_scr[...] = jnp.zeros(acc_scr.shape, f32)

    def step(masked):
        k = k_ref[0]
        kt = kt_ref[0]
        heads = range(MLA_HEADS)
        sts = [lax.dot_general(k, q_ref[0, h], (((1,), (1,)), ((), ())), preferred_element_type=f32)
               for h in heads]
        if masked:
            keep = (lax.broadcasted_iota(jnp.int32, (tk, tq), 0) <= lax.broadcasted_iota(jnp.int32, (tk, tq), 1))
            sts = [jnp.where(keep, st, -jnp.inf) for st in sts]
        alphas, pts = [], []
        for h in heads:
            m_prev = m_scr[h]
            m_new = jnp.maximum(m_prev, jnp.max(sts[h], 0, keepdims=True))
            alphas.append(jnp.exp2((m_prev - m_new) * EXP2_SCALE))
            pts.append(jnp.exp2((sts[h] - m_new) * EXP2_SCALE).astype(bf16))
            m_scr[h] = m_new
        for h in heads:
            acc_scr[h] = alphas[h] * acc_scr[h] + jnp.dot(kt, pts[h], preferred_element_type=f32)

    @pl.when(ki < qi)
    def _():
        step(False)

    @pl.when(ki == qi)
    def _():
        step(True)
        yt = jnp.zeros((MLA_HEADS * MLA_DV, tq), f32)
        for h in range(MLA_HEADS):
            acc = acc_scr[h]
            o_t = acc[0:MLA_KV_LORA, :] / acc[QK_W - 1:QK_W, :]
            yt = yt + jnp.dot(wuvt_ref[h], o_t.astype(bf16), preferred_element_type=f32)
        y_ref[0] = yt.T.astype(bf16)


def _mla_flash(qcat, kcat, kcat_t, lw, layer, tq):
    nseq, _, lp, _ = qcat.shape
    tk = tq
    nq = lp // tq
    pairs = [(i, j) for i in range(nq) for j in range(i + 1)]
    qi_tab = jnp.asarray([p[0] for p in pairs], jnp.int32)
    ki_tab = jnp.asarray([p[1] for p in pairs], jnp.int32)
    wuvt = lw["mla_wuvt_x"]
    grid_spec = pltpu.PrefetchScalarGridSpec(
        num_scalar_prefetch=2,
        grid=(nseq, len(pairs)),
        in_specs=[pl.BlockSpec((1, MLA_HEADS, tq, QK_W), lambda b, t, qi, ki: (b, 0, qi[t], 0)),
                  pl.BlockSpec((1, tk, QK_W), lambda b, t, qi, ki: (b, ki[t], 0)),
                  pl.BlockSpec((1, QK_W, tk), lambda b, t, qi, ki: (b, 0, ki[t])),
                  pl.BlockSpec((None,) + wuvt.shape[1:], lambda b, t, qi, ki: (layer, 0, 0, 0))],
        out_specs=pl.BlockSpec((1, tq, MLA_HEADS * MLA_DV), lambda b, t, qi, ki: (b, qi[t], 0)),
        scratch_shapes=[pltpu.VMEM((MLA_HEADS, 1, tq), f32), pltpu.VMEM((MLA_HEADS, QK_W, tq), f32)],
    )
    return pl.pallas_call(
        functools.partial(_mla_flash_kernel, tq, tk),
        grid_spec=grid_spec,
        out_shape=jax.ShapeDtypeStruct((nseq, lp, MLA_HEADS * MLA_DV), bf16),
        compiler_params=_params(("parallel", "arbitrary")),
        name="mla_flash",
    )(qi_tab, ki_tab, qcat, kcat, kcat_t, wuvt)


def _mla_paged_kernel(valid, n_pages, layer, pt_ref, q_ref, knew_ref, wuv_ref, kv_hbm, krt_hbm, y_ref,
                      kv_buf, krt_buf, kvb_scr, s_scr, sem):
    b = pl.program_id(0)
    nb = pl.num_programs(0)
    m_rows = MLA_HEADS * SEG_PAD
    t_past = n_pages * PAGE_SIZE
    chunk = math.gcd(t_past, PAGED_CHUNK)

    def page_copies(seq, i, slot):
        pg = pt_ref[seq, i]
        rows = pl.ds(i * PAGE_SIZE, PAGE_SIZE)
        return (pltpu.make_async_copy(kv_hbm.at[layer, pg], kv_buf.at[slot, rows, :], sem.at[0, slot]),
                pltpu.make_async_copy(krt_hbm.at[layer, pg], krt_buf.at[slot, :, rows], sem.at[1, slot]))

    def start_seq(seq, slot):
        for i in range(n_pages):
            kv_cp, kr_cp = page_copies(seq, i, slot)
            kv_cp.start(priority=0)
            kr_cp.start(priority=1)

    def wait_seq(seq, slot):
        for i in range(n_pages):
            for cp in page_copies(seq, i, slot):
                cp.wait()

    @pl.when(b == 0)
    def _():
        start_seq(0, 0)
        start_seq(jnp.minimum(1, nb - 1), 1)

    slot = b % PAGED_SLOTS
    wait_seq(b, slot)
    start_seq(jnp.minimum(b + 2, nb - 1), (b + 2) % PAGED_SLOTS)

    q = q_ref[0].reshape(m_rows, QK_W)
    q_lat = q[:, 0:MLA_KV_LORA]
    q_rope = q[:, MLA_KV_LORA:MLA_KV_LORA + MLA_ROPE]
    for ci in range(t_past // chunk):
        cols = pl.ds(ci * chunk, chunk)
        kvc = kv_buf[slot, cols, :].astype(bf16)
        kvb_scr[cols, :] = kvc
        s_scr[:, cols] = (lax.dot_general(q_lat, kvc, (((1,), (1,)), ((), ())), preferred_element_type=f32)
                          + jnp.dot(q_rope, krt_buf[slot, :, cols].astype(bf16), preferred_element_type=f32))

    knew = knew_ref[0]
    sn = lax.dot_general(q, knew, (((1,), (1,)), ((), ())), preferred_element_type=f32)
    rpos = lax.broadcasted_iota(jnp.int32, (MLA_HEADS, SEG_PAD, SEG_PAD), 1).reshape(m_rows, SEG_PAD)
    cpos = lax.broadcasted_iota(jnp.int32, (m_rows, SEG_PAD), 1)
    sn = jnp.where((cpos <= rpos) & (cpos < valid), sn, -jnp.inf)
    m = jnp.maximum(jnp.max(s_scr[...], -1, keepdims=True), jnp.max(sn, -1, keepdims=True))
    pn = jnp.exp((sn - m) * MLA_SCALE)
    l = jnp.sum(pn, -1, keepdims=True)
    acc = jnp.dot(pn.astype(bf16), knew[:, 0:MLA_KV_LORA], preferred_element_type=f32)
    for ci in range(t_past // chunk):
        cols = pl.ds(ci * chunk, chunk)
        pr = jnp.exp((s_scr[:, cols] - m) * MLA_SCALE)
        l = l + jnp.sum(pr, -1, keepdims=True)
        acc = acc + jnp.dot(pr.astype(bf16), kvb_scr[cols, :], preferred_element_type=f32)
    o_lat = acc / l
    y = jnp.zeros((SEG_PAD, MLA_HEADS * MLA_DV), f32)
    for h in range(MLA_HEADS):
        y = y + _bdot(o_lat[h * SEG_PAD:(h + 1) * SEG_PAD, :], wuv_ref[h])
    y_ref[0] = y.astype(bf16)

    @pl.when(b == nb - 1)
    def _():
        wait_seq(b, (b + 1) % PAGED_SLOTS)
        wait_seq(b, (b + 2) % PAGED_SLOTS)


def _mla_paged(qcat, kcat, cache_kv, cache_krt, page_table, lw, layer, valid):
    nseq = qcat.shape[0]
    n_pages = page_table.shape[1]
    t_past = n_pages * PAGE_SIZE
    wuv = lw["mla_wuv_x"]
    grid_spec = pltpu.PrefetchScalarGridSpec(
        num_scalar_prefetch=1,
        grid=(nseq,),
        in_specs=[pl.BlockSpec((1, MLA_HEADS, SEG_PAD, QK_W), lambda b, pt: (b, 0, 0, 0)),
                  pl.BlockSpec((1, SEG_PAD, QK_W), lambda b, pt: (b, 0, 0)),
                  pl.BlockSpec((None,) + wuv.shape[1:], lambda b, pt: (layer, 0, 0, 0)),
                  pl.BlockSpec(memory_space=pl.ANY), pl.BlockSpec(memory_space=pl.ANY)],
        out_specs=pl.BlockSpec((1, SEG_PAD, MLA_HEADS * MLA_DV), lambda b, pt: (b, 0, 0)),
        scratch_shapes=[pltpu.VMEM((PAGED_SLOTS, t_past, MLA_KV_LORA), f32),
                        pltpu.VMEM((PAGED_SLOTS, MLA_ROPE, t_past), f32),
                        pltpu.VMEM((t_past, MLA_KV_LORA), bf16),
                        pltpu.VMEM((MLA_HEADS * SEG_PAD, t_past), f32),
                        pltpu.SemaphoreType.DMA((2, PAGED_SLOTS))],
    )
    return pl.pallas_call(
        functools.partial(_mla_paged_kernel, valid, n_pages, layer),
        grid_spec=grid_spec,
        out_shape=jax.ShapeDtypeStruct((nseq, SEG_PAD, MLA_HEADS * MLA_DV), bf16),
        compiler_params=_params(("arbitrary",)),
        name="mla_paged",
    )(page_table, qcat, kcat, wuv, cache_kv, cache_krt)


def _ffn_kernel(nsb, ts, valid, x_ref, ys_ref, yg_ref, ym_ref, pe_ref, halo0_ref,
                wout_ref, g1_ref, b1_ref, wup_ref, fcw_ref, fcb_ref, wdn_ref, wpg_ref, wpp_ref, g2_ref, b2_ref,
                o_ref, halo_out_ref, halo_scr, hmid_scr):
    ti = pl.program_id(1)
    rows = nsb * ts

    @pl.when(ti == 0)
    def _():
        halo_scr[...] = halo0_ref[...]

    x = x_ref[...].reshape(rows, D_MODEL)
    ycat = jnp.concatenate([ys_ref[...].reshape(rows, SSD_WIDTH), yg_ref[...].reshape(rows, GLA_WIDTH),
                            ym_ref[...].reshape(rows, MLA_HEADS * MLA_DV)], axis=-1)
    mix = jnp.dot(ycat, wout_ref[...], preferred_element_type=f32)
    x1 = _layernorm(DN_ALPHA * x + mix, g1_ref[...], b1_ref[...])
    x1b = x1.astype(bf16)

    pos = lax.broadcasted_iota(jnp.int32, (nsb, ts, 1), 1)
    for ch in range(D_FF // FF_CHUNK):
        lo = ch * FF_CHUNK
        a = jnp.dot(x1b, wup_ref[:, lo:lo + FF_CHUNK], preferred_element_type=f32)
        bgate = jnp.dot(x1b, wup_ref[:, D_FF + lo:D_FF + lo + FF_CHUNK], preferred_element_type=f32)
        a3 = a.reshape(nsb, ts, FF_CHUNK)
        halo = halo_scr[:, :, lo:lo + FF_CHUNK]
        hm1, hm2 = halo[:, 7:8, :], halo[:, 6:7, :]
        s1 = jnp.where(pos >= 1, pltpu.roll(a3, 1, axis=1), hm1)
        s2 = jnp.where(pos >= 2, pltpu.roll(a3, 2, axis=1), jnp.where(pos == 1, hm1, hm2))
        fcw = fcw_ref[:, lo:lo + FF_CHUNK]
        ac = fcw[2:3, :] * a3 + fcw[1:2, :] * s1 + fcw[0:1, :] * s2 + fcb_ref[:, lo:lo + FF_CHUNK]
        last = a3[:, ts - 8:ts, :]
        if valid < ts:
            last = pltpu.roll(last, ts - valid, axis=1)
        halo_scr[:, :, lo:lo + FF_CHUNK] = last
        ge = 0.5 * ac * (1.0 + lax.erf(ac * (1.0 / math.sqrt(2.0))))
        hmid_scr[:, lo:lo + FF_CHUNK] = (ge.reshape(rows, FF_CHUNK) * bgate).astype(bf16)
    f = jnp.dot(hmid_scr[...], wdn_ref[...], preferred_element_type=f32)

    gate = jax.nn.sigmoid(jnp.dot(x1b, wpg_ref[...], preferred_element_type=f32))
    pe = _bdot(pe_ref[...].reshape(rows, PLE_DIM), wpp_ref[...])
    x2 = _layernorm(DN_ALPHA * x1 + f + gate * pe, g2_ref[...], b2_ref[...])
    o_ref[...] = x2.reshape(nsb, ts, D_MODEL)

    @pl.when(ti == pl.num_programs(1) - 1)
    def _():
        halo_out_ref[...] = halo_scr[...]


def _ffn(x3, ys, yg, ym, pe4, halo0, st_layer, lw, layer, nsb, ts, valid):
    nseq, lp, _ = x3.shape

    def tok(width):
        return pl.BlockSpec((nsb, ts, width), lambda b, i: (b, i, 0))

    names = ["w_out", "ln1_g", "ln1_b", "ffn_w_up", "ffn_conv_w", "ffn_conv_b", "ffn_w_down", "ple_w_gate",
             "ple_w_proj", "ln2_g", "ln2_b"]
    return pl.pallas_call(
        functools.partial(_ffn_kernel, nsb, ts, valid),
        grid=(nseq // nsb, lp // ts),
        in_specs=[tok(D_MODEL), tok(SSD_WIDTH), tok(GLA_WIDTH), tok(MLA_HEADS * MLA_DV),
                  pl.BlockSpec((None, nsb, ts, PLE_DIM), lambda b, i: (layer, b, i, 0)),
                  pl.BlockSpec((None, nsb, 8, D_FF), lambda b, i: (st_layer, b, 0, 0))]
        + [_layer_spec(lw[n], layer) for n in names],
        out_specs=[tok(D_MODEL), pl.BlockSpec((nsb, 8, D_FF), lambda b, i: (b, 0, 0))],
        out_shape=[jax.ShapeDtypeStruct((nseq, lp, D_MODEL), f32),
                   jax.ShapeDtypeStruct((nseq, 8, D_FF), f32)],
        scratch_shapes=[pltpu.VMEM((nsb, 8, D_FF), f32), pltpu.VMEM((nsb * ts, D_FF), bf16)],
        compiler_params=_params(("parallel", "arbitrary")),
        name="out_ffn",
    )(x3, ys, yg, ym, pe4, halo0, *[lw[n] for n in names])


def _prep_weights(w):
    depth = w["w_in"].shape[0]

    def row(v):
        return v.reshape(depth, 1, -1).astype(f32)

    splits = np.cumsum([0, SSD_WIDTH, SSD_CONV_CH, SSD_HEADS, GLA_HEADS * GLA_DK, GLA_HEADS * GLA_DK, GLA_WIDTH,
                        GLA_WIDTH, GLA_GATE_RANK, MLA_Q_LORA, MLA_KV_LORA, MLA_ROPE]).tolist()
    w_in = w["w_in"]
    (z, xbc, dt, q, k, v, r, glr, cq, ckv, kr) = [w_in[:, :, splits[j]:splits[j + 1]] for j in range(11)]

    def zpad(n):
        return jnp.zeros((depth, D_MODEL, n), f32)

    half = MLA_ROPE // 2
    w_all = jnp.concatenate(
        [z, xbc, dt, zpad(P_SSD - SSD_WIDTH - SSD_CONV_CH - SSD_HEADS),
         q, k, v, r, glr, zpad(P_GLA - 2 * GLA_HEADS * GLA_DK - 2 * GLA_WIDTH - GLA_GATE_RANK),
         cq, zpad(256 - MLA_Q_LORA), ckv, kr[:, :, :half], zpad(128 - half), kr[:, :, half:], zpad(128 - half)],
        axis=2).astype(bf16)

    wuq = w["mla_w_uq"].reshape(depth, MLA_Q_LORA, MLA_HEADS, MLA_NOPE + MLA_ROPE)
    nope = wuq[..., :MLA_NOPE].reshape(depth, MLA_Q_LORA, MLA_HEADS * MLA_NOPE)
    ra = wuq[..., MLA_NOPE:MLA_NOPE + half].reshape(depth, MLA_Q_LORA, MLA_HEADS * half)
    rb = wuq[..., MLA_NOPE + half:].reshape(depth, MLA_Q_LORA, MLA_HEADS * half)
    zq = jnp.zeros((depth, MLA_Q_LORA, 128 - MLA_HEADS * half), f32)
    wuq_x = jnp.concatenate([nope, ra, zq, rb, zq], axis=2)
    wuq_x = jnp.concatenate([wuq_x, jnp.zeros((depth, 256 - MLA_Q_LORA, 512), f32)], axis=1).astype(bf16)

    wuk = w["mla_w_uk"]
    wcat = jnp.zeros((depth, MLA_HEADS, 512, QK_W), f32)
    eye = jnp.eye(half, dtype=f32)
    for h in range(MLA_HEADS):
        wcat = wcat.at[:, h, h * MLA_NOPE:(h + 1) * MLA_NOPE, 0:MLA_KV_LORA].set(jnp.swapaxes(wuk[:, :, h, :], 1, 2))
        wcat = wcat.at[:, h, 256 + h * half:256 + (h + 1) * half, MLA_KV_LORA:MLA_KV_LORA + half].set(eye)
        wcat = wcat.at[:, h, 384 + h * half:384 + (h + 1) * half,
                       MLA_KV_LORA + half:MLA_KV_LORA + 2 * half].set(eye)
    sel = jnp.zeros((256, 128), f32)
    sel = sel.at[0:half, 0:half].set(eye).at[128:128 + half, half:2 * half].set(eye)
    wuv = w["mla_w_uv"]
    wuv_x = jnp.zeros((depth, MLA_HEADS, MLA_KV_LORA, MLA_HEADS * MLA_DV), f32)
    for h in range(MLA_HEADS):
        wuv_x = wuv_x.at[:, h, :, h * MLA_DV:(h + 1) * MLA_DV].set(wuv[:, :, h, :])

    return {
        "w_in": w_all,
        "ssd_conv_w": w["ssd_conv_w"], "ssd_conv_b": row(w["ssd_conv_b"]),
        "ssd_dt_bias": row(w["ssd_dt_bias"]), "ssd_a_log": row(w["ssd_a_log"]),
        "ssd_d_x": row(jnp.repeat(w["ssd_d"], HEAD_DIM, axis=1)), "ssd_norm_g": row(w["ssd_norm_g"]),
        "gla_gate_w": w["gla_gate_w"], "gla_gate_b": row(w["gla_gate_b"]),
        "gla_norm_g_x": row(jnp.tile(w["gla_norm_g"], (1, GLA_HEADS))),
        "mla_q_norm_g_x": row(jnp.concatenate([w["mla_q_norm_g"], jnp.zeros((depth, 256 - MLA_Q_LORA), f32)], axis=1)),
        "mla_kv_norm_g": row(w["mla_kv_norm_g"]),
        "mla_w_uq_x": wuq_x, "mla_wcat": wcat.astype(bf16), "mla_sel": sel, "mla_wuv_x": wuv_x.astype(bf16),
        "mla_wuvt_x": jnp.swapaxes(wuv_x, 2, 3).astype(bf16),
        "w_out": w["w_out"].astype(bf16), "ln1_g": row(w["ln1_g"]), "ln1_b": row(w["ln1_b"]),
        "ffn_w_up": w["ffn_w_up"].astype(bf16), "ffn_conv_w": w["ffn_conv_w"],
        "ffn_conv_b": row(w["ffn_conv_b"]), "ffn_w_down": w["ffn_w_down"].astype(bf16),
        "ple_w_gate": w["ple_w_gate"].astype(bf16), "ple_w_proj": w["ple_w_proj"].astype(bf16),
        "ln2_g": row(w["ln2_g"]), "ln2_b": row(w["ln2_b"]),
    }


def _rope_tables(pos, rows):
    half = MLA_ROPE // 2
    inv = ROPE_THETA ** (-jnp.arange(half, dtype=f32) / half)
    ang = pos.astype(f32)[:, None] * inv
    pad = jnp.zeros((pos.shape[0], 128 - MLA_HEADS * half), f32)
    cos = jnp.concatenate([jnp.tile(jnp.cos(ang), (1, MLA_HEADS)), pad], axis=1)
    sin = jnp.concatenate([jnp.tile(jnp.sin(ang), (1, MLA_HEADS)), pad], axis=1)
    reps = max(1, rows // pos.shape[0])
    return jnp.tile(cos, (reps, 1)), jnp.tile(sin, (reps, 1))


def _halo(buf):
    lead = buf.shape[:-2]
    wdt, ch = buf.shape[-2:]
    return jnp.concatenate([jnp.zeros(lead + (8 - wdt, ch), f32), buf], axis=-2)


def _layer(x3, pe4, layer, states, st_layer, lw, cfg, cache=None):
    nseq, lp, _ = x3.shape
    c, nsq, valid, ts, ts_ffn, nsb = cfg["c"], cfg["nsq"], cfg["valid"], cfg["ts"], cfg["ts_ffn"], cfg["nsb"]
    ssm0, conv_halo0, gla0, ffn_halo0 = states
    rows = nseq * lp
    p_ssd, p_gla, p_mla = _in_proj(x3.reshape(rows, D_MODEL), lw, layer, min(rows, 1024))
    y_ssd, ssm_h, conv_halo = _ssd(p_ssd.reshape(nseq, lp, P_SSD), conv_halo0, ssm0, st_layer, lw, layer, nsq, c,
                                   valid)
    y_gla, gla_s = _gla(p_gla.reshape(nseq, lp, P_GLA), gla0, st_layer, lw, layer, nsq, c, valid)
    prep = _mla_prep(p_mla.reshape(nseq, lp, P_MLA), cfg["cos"], cfg["sin"], lw, layer, nsb, ts, cache is None)
    qcat, kcat, c_kv, k_rope = prep[:4]
    if cache is None:
        y_mla = _mla_flash(qcat, kcat, prep[4], lw, layer, ts)
    else:
        cache_kv, cache_krt, page_table = cache
        y_mla = _mla_paged(qcat, kcat, cache_kv, cache_krt, page_table, lw, layer, valid)
    x_out, ffn_halo = _ffn(x3, y_ssd, y_gla, y_mla, pe4, ffn_halo0, st_layer, lw, layer, nsb, ts_ffn, valid)
    out_states = (c_kv[:, :valid], k_rope[:, :valid], ssm_h, conv_halo[:, 8 - (SSD_CONV - 1):], gla_s,
                  ffn_halo[:, 8 - (FFN_CONV - 1):])
    return x_out, out_states


def _pad_rows(a, lp):
    nseq, l = a.shape[:2]
    if l == lp:
        return a
    return jnp.concatenate([a, jnp.zeros((nseq, lp - l) + a.shape[2:], a.dtype)], axis=1)


def kernel(x_prompt, x_sample, cache_kv_latent, cache_k_rope, state_ssm, state_ssm_conv, state_gla, state_ffn_conv, page_table, p_prompt, p_sample, ln_in_g, ln_in_b, w_in, ssd_conv_w, ssd_conv_b, ssd_dt_bias, ssd_a_log, ssd_d, ssd_norm_g, gla_gate_w, gla_gate_b, gla_norm_g, mla_q_norm_g, mla_w_uq, mla_kv_norm_g, mla_w_uk, mla_w_uv, w_out, ln1_g, ln1_b, ffn_w_up, ffn_conv_w, ffn_conv_b, ffn_w_down, ple_w_proj, ple_w_gate, ln2_g, ln2_b):
    weights = dict(w_in=w_in, ssd_conv_w=ssd_conv_w, ssd_conv_b=ssd_conv_b, ssd_dt_bias=ssd_dt_bias,
                   ssd_a_log=ssd_a_log, ssd_d=ssd_d, ssd_norm_g=ssd_norm_g, gla_gate_w=gla_gate_w,
                   gla_gate_b=gla_gate_b, gla_norm_g=gla_norm_g, mla_q_norm_g=mla_q_norm_g, mla_w_uq=mla_w_uq,
                   mla_kv_norm_g=mla_kv_norm_g, mla_w_uk=mla_w_uk, mla_w_uv=mla_w_uv, w_out=w_out, ln1_g=ln1_g,
                   ln1_b=ln1_b, ffn_w_up=ffn_w_up, ffn_conv_w=ffn_conv_w, ffn_conv_b=ffn_conv_b,
                   ffn_w_down=ffn_w_down, ple_w_proj=ple_w_proj, ple_w_gate=ple_w_gate, ln2_g=ln2_g, ln2_b=ln2_b)
    depth = w_in.shape[0]
    bp, lp_, _ = x_prompt.shape
    bs, ls, _ = x_sample.shape
    t_past = page_table.shape[1] * PAGE_SIZE
    lw = _prep_weights(weights)

    g_in = ln_in_g.reshape(1, -1)
    b_in = ln_in_b.reshape(1, -1)
    hp = _ln_in(x_prompt.reshape(bp * lp_, D_MODEL), g_in, b_in, min(bp * lp_, 1024)).reshape(bp, lp_, D_MODEL)
    xs_pad = _pad_rows(x_sample, SEG_PAD)
    hs = _ln_in(xs_pad.reshape(bs * SEG_PAD, D_MODEL), g_in, b_in, bs * SEG_PAD).reshape(bs, SEG_PAD, D_MODEL)

    ts_p = min(lp_, 512)
    cos_p, sin_p = _rope_tables(jnp.arange(lp_), ts_p)
    cfg_p = dict(c=min(lp_, 128), nsq=math.gcd(bp, 4), valid=lp_, ts=ts_p, ts_ffn=ts_p, nsb=1,
                 cos=cos_p, sin=sin_p)
    cos_s, sin_s = _rope_tables(t_past + jnp.arange(SEG_PAD), bs * SEG_PAD)
    cfg_s = dict(c=SEG_PAD, nsq=math.gcd(bs, 8), valid=ls, ts=SEG_PAD, ts_ffn=SEG_PAD, nsb=bs, cos=cos_s, sin=sin_s)

    states_p = (jnp.zeros((1, bp, SSD_HEADS, HEAD_DIM, SSD_STATE), f32), jnp.zeros((1, bp, 8, SSD_CONV_CH), f32),
                jnp.zeros((1, bp, GLA_HEADS, GLA_DK, GLA_DV), f32), jnp.zeros((1, bp, 8, D_FF), f32))
    states_s = (state_ssm, _halo(state_ssm_conv), state_gla, _halo(state_ffn_conv))
    pe_s = jnp.concatenate([p_sample, jnp.zeros((depth, bs, SEG_PAD - ls, PLE_DIM), f32)], axis=2)
    cache_krt = jnp.swapaxes(cache_k_rope, 2, 3)

    prompt_states, sample_states = [], []
    for i in range(depth):
        hp, sp = _layer(hp, p_prompt, i, states_p, 0, lw, cfg_p)
        hs, ss = _layer(hs, pe_s, i, states_s, i, lw, cfg_s, cache=(cache_kv_latent, cache_krt, page_table))
        prompt_states.append(sp)
        sample_states.append(ss)
    kv_p, kr_p, ssm_p, ssm_conv_p, gla_p, ffn_conv_p = (jnp.stack(f) for f in zip(*prompt_states))
    kv_s, kr_s, ssm_s, ssm_conv_s, gla_s, ffn_conv_s = (jnp.stack(f) for f in zip(*sample_states))
    return (hp, hs[:, :ls], kv_p, kv_s, kr_p, kr_s, ssm_p, ssm_s, ssm_conv_p, ssm_conv_s, gla_p, gla_s,
            ffn_conv_p, ffn_conv_s)
```

```python
import functools
import math

import jax
import jax.numpy as jnp
import numpy as np
from jax import lax
from jax.experimental import pallas as pl
from jax.experimental.pallas import tpu as pltpu

f32 = jnp.float32
bf16 = jnp.bfloat16

D_MODEL = 1024
HEAD_DIM = 64
SSD_WIDTH = 512
SSD_HEADS = 8
SSD_GROUPS = 2
SSD_STATE = 128
SSD_CONV = 4
SSD_CONV_CH = SSD_WIDTH + 2 * SSD_GROUPS * SSD_STATE
GLA_WIDTH = 256
GLA_HEADS = 4
GLA_DK = 32
GLA_DV = 64
GLA_GATE_RANK = 16
GLA_TAU = 16.0
MLA_HEADS = 4
MLA_NOPE = 64
MLA_ROPE = 32
MLA_DV = 64
MLA_Q_LORA = 192
MLA_KV_LORA = 128
MLA_SCALE = (MLA_NOPE + MLA_ROPE) ** -0.5
EXP2_SCALE = MLA_SCALE * math.log2(math.e)
ROPE_THETA = 10000.0
D_FF = 2816
FFN_CONV = 3
PLE_DIM = 256
PAGE_SIZE = 128
DEPTH = 4
DN_ALPHA = (2 * DEPTH) ** 0.25

P_SSD = 1664
P_GLA = 896
P_MLA = 640
QK_W = 256

SEG_PAD = 8
FF_CHUNK = 256
PAGED_SLOTS = 3
PAGED_CHUNK = 2048
GLA_SAFE_LOG_DECAY = -60.0
VMEM_LIMIT = 60 * 1024 * 1024


def _bdot(a, b):
    return jnp.dot(a.astype(bf16), b.astype(bf16), preferred_element_type=f32)


def _bdot_nt(a, b):
    return lax.dot_general(a.astype(bf16), b.astype(bf16), (((1,), (1,)), ((), ())),
                           preferred_element_type=f32)


def _bdot_tn(a, b):
    return lax.dot_general(a.astype(bf16), b.astype(bf16), (((0,), (0,)), ((), ())),
                           preferred_element_type=f32)


def _split3(a):
    hi = a.astype(bf16)
    r1 = a - hi.astype(f32)
    mid = r1.astype(bf16)
    lo = (r1 - mid.astype(f32)).astype(bf16)
    return hi, mid, lo


def _dot_f32_lhs(a, m):
    m = m.astype(bf16)
    hi, mid, lo = _split3(a)
    return (jnp.dot(hi, m, preferred_element_type=f32) + jnp.dot(mid, m, preferred_element_type=f32)
            + jnp.dot(lo, m, preferred_element_type=f32))


def _dot_f32_rhs(m, a):
    m = m.astype(bf16)
    hi, mid, lo = _split3(a)
    return (jnp.dot(m, hi, preferred_element_type=f32) + jnp.dot(m, mid, preferred_element_type=f32)
            + jnp.dot(m, lo, preferred_element_type=f32))


def _dot_split2(a, b):
    a_hi = a.astype(bf16)
    a_lo = (a - a_hi.astype(f32)).astype(bf16)
    b_hi = b.astype(bf16)
    b_lo = (b - b_hi.astype(f32)).astype(bf16)
    return (jnp.dot(a_hi, b_hi, preferred_element_type=f32) + jnp.dot(a_hi, b_lo, preferred_element_type=f32)
            + jnp.dot(a_lo, b_hi, preferred_element_type=f32))


def _silu(x):
    return x * jax.nn.sigmoid(x)


def _softplus(x):
    return jnp.maximum(x, 0.0) + jnp.log1p(jnp.exp(-jnp.abs(x)))


def _log_sigmoid(x):
    return jnp.minimum(x, 0.0) - jnp.log1p(jnp.exp(-jnp.abs(x)))


def _layernorm(x, g, b, eps=1e-5):
    xc = x - jnp.mean(x, -1, keepdims=True)
    var = jnp.mean(xc * xc, -1, keepdims=True)
    return xc * lax.rsqrt(var + eps) * g + b


def _tri(c):
    r = lax.broadcasted_iota(jnp.int32, (c, c), 0)
    col = lax.broadcasted_iota(jnp.int32, (c, c), 1)
    return col <= r


def _params(sem):
    return pltpu.CompilerParams(dimension_semantics=sem, vmem_limit_bytes=VMEM_LIMIT)


def _const_spec(shape):
    nd = len(shape)
    return pl.BlockSpec(shape, lambda *_: (0,) * nd)


def _layer_spec(arr, layer):
    shape = arr.shape[1:]
    nd = len(shape)
    return pl.BlockSpec((None,) + shape, lambda *_: (layer,) + (0,) * nd, pipeline_mode=pl.Buffered(1))


def _ln_kernel(x_ref, g_ref, b_ref, o_ref):
    o_ref[...] = _layernorm(x_ref[...], g_ref[...], b_ref[...])


def _ln_in(x2, g, b, tm):
    rows = x2.shape[0]
    return pl.pallas_call(
        _ln_kernel,
        grid=(rows // tm,),
        in_specs=[pl.BlockSpec((tm, D_MODEL), lambda i: (i, 0)), _const_spec((1, D_MODEL)), _const_spec((1, D_MODEL))],
        out_specs=pl.BlockSpec((tm, D_MODEL), lambda i: (i, 0)),
        out_shape=jax.ShapeDtypeStruct((rows, D_MODEL), f32),
        compiler_params=_params(("parallel",)),
        name="ln_in",
    )(x2, g, b)


def _in_proj_kernel(x_ref, w_ref, ssd_ref, gla_ref, mla_ref):
    x = x_ref[...].astype(bf16)
    ssd_ref[...] = jnp.dot(x, w_ref[:, 0:P_SSD], preferred_element_type=f32)
    gla_ref[...] = jnp.dot(x, w_ref[:, P_SSD:P_SSD + P_GLA], preferred_element_type=f32)
    mla_ref[...] = jnp.dot(x, w_ref[:, P_SSD + P_GLA:], preferred_element_type=f32)


def _in_proj(x2, lw, layer, tm):
    rows = x2.shape[0]
    return pl.pallas_call(
        _in_proj_kernel,
        grid=(rows // tm,),
        in_specs=[pl.BlockSpec((tm, D_MODEL), lambda i: (i, 0)), _layer_spec(lw["w_in"], layer)],
        out_specs=[pl.BlockSpec((tm, P_SSD), lambda i: (i, 0)),
                   pl.BlockSpec((tm, P_GLA), lambda i: (i, 0)),
                   pl.BlockSpec((tm, P_MLA), lambda i: (i, 0))],
        out_shape=[jax.ShapeDtypeStruct((rows, P_SSD), f32),
                   jax.ShapeDtypeStruct((rows, P_GLA), f32),
                   jax.ShapeDtypeStruct((rows, P_MLA), f32)],
        compiler_params=_params(("parallel",)),
        name="in_proj",
    )(x2, lw["w_in"])


def _ssd_kernel(nsq, c, valid, p_ref, halo0_ref, h0_ref, cw_ref, cb_ref, dtb_ref, alog_ref, dexp_ref, ng_ref,
                y_ref, hout_ref, halo_out_ref, ht_scr, halo_scr):
    ci = pl.program_id(1)
    hp = SSD_HEADS * HEAD_DIM
    gw = hp // SSD_GROUPS
    heads_per_group = SSD_HEADS // SSD_GROUPS

    @pl.when(ci == 0)
    def _():
        for s in range(nsq):
            ht_scr[s] = h0_ref[s].reshape(hp, SSD_STATE).T
        halo_scr[...] = halo0_ref[...]

    row = lax.broadcasted_iota(jnp.int32, (c, 1), 0)
    tri = _tri(c)
    tri_b = tri.astype(bf16)
    expand = (lax.broadcasted_iota(jnp.int32, (SSD_HEADS, hp), 1) // HEAD_DIM
              == lax.broadcasted_iota(jnp.int32, (SSD_HEADS, hp), 0)).astype(bf16)
    lane_head = lax.broadcasted_iota(jnp.int32, (1, gw), 1) // HEAD_DIM
    cw = cw_ref[...]
    a_neg = -jnp.exp(alog_ref[...])

    sq = range(nsq)
    groups = range(SSD_GROUPS)
    gsl = [slice(g * gw, (g + 1) * gw) for g in groups]
    zs, xss, bss, css, dts, las = [], [], [], [], [], []
    for s in sq:
        p = p_ref[s]
        xr = p[:, SSD_WIDTH:SSD_WIDTH + SSD_CONV_CH]
        dt_raw = p[:, SSD_WIDTH + SSD_CONV_CH:SSD_WIDTH + SSD_CONV_CH + SSD_HEADS]
        halo = halo_scr[s]
        hm1, hm2, hm3 = halo[7:8, :], halo[6:7, :], halo[5:6, :]
        s1 = jnp.where(row >= 1, pltpu.roll(xr, 1, axis=0), hm1)
        s2 = jnp.where(row >= 2, pltpu.roll(xr, 2, axis=0), jnp.where(row == 1, hm1, hm2))
        s3 = jnp.where(row >= 3, pltpu.roll(xr, 3, axis=0),
                       jnp.where(row == 2, hm1, jnp.where(row == 1, hm2, hm3)))
        conv = cw[3:4, :] * xr + cw[2:3, :] * s1 + cw[1:2, :] * s2 + cw[0:1, :] * s3 + cb_ref[...]
        last = xr[c - 8:c, :]
        if valid < c:
            last = pltpu.roll(last, c - valid, axis=0)
        halo_scr[s] = last
        xbc = _silu(conv)
        dt = _softplus(dt_raw + dtb_ref[...])
        if valid < c:
            dt = jnp.where(row < valid, dt, 0.0)
        zs.append(p[:, 0:SSD_WIDTH])
        xss.append(xbc[:, 0:SSD_WIDTH])
        bss.append(xbc[:, SSD_WIDTH:SSD_WIDTH + SSD_GROUPS * SSD_STATE])
        css.append(xbc[:, SSD_WIDTH + SSD_GROUPS * SSD_STATE:])
        dts.append(dt)
        las.append(dt * a_neg)

    cums = [_dot_f32_rhs(tri_b, las[s]) for s in sq]
    ecum_xs = [_dot_f32_lhs(jnp.exp(cums[s]), expand) for s in sq]
    w_xs = [_dot_f32_lhs(jnp.exp(cums[s][c - 1:c, :] - cums[s]) * dts[s], expand) for s in sq]
    hts = [ht_scr[s] for s in sq]
    b_g = [[bss[s][:, g * SSD_STATE:(g + 1) * SSD_STATE] for g in groups] for s in sq]
    c_g = [[css[s][:, g * SSD_STATE:(g + 1) * SSD_STATE] for g in groups] for s in sq]
    scores = [[_bdot_nt(c_g[s][g], b_g[s][g]) for g in groups] for s in sq]
    y_state = [[_bdot(c_g[s][g], hts[s][:, gsl[g]]) for g in groups] for s in sq]
    upd = [[_bdot_tn(b_g[s][g], xss[s][:, gsl[g]] * w_xs[s][:, gsl[g]]) for g in groups] for s in sq]
    for s in sq:
        for g in groups:
            ht_scr[s, :, gsl[g]] = hts[s][:, gsl[g]] * ecum_xs[s][c - 1:c, gsl[g]] + upd[s][g]

    ms = []
    for s in sq:
        cum_t = cums[s].T
        dt_t = dts[s].T
        per_head = []
        for h in range(SSD_HEADS):
            seg = cums[s][:, h:h + 1] - cum_t[h:h + 1, :]
            decay = jnp.exp(jnp.where(tri, seg, -jnp.inf))
            per_head.append((scores[s][h // heads_per_group] * decay * dt_t[h:h + 1, :]).astype(bf16))
        ms.append(per_head)
    ys = []
    for s in sq:
        y_parts = []
        for g in groups:
            x_g = xss[s][:, gsl[g]]
            y_g = y_state[s][g] * ecum_xs[s][:, gsl[g]]
            for hh in range(heads_per_group):
                xm = jnp.where(lane_head == hh, x_g, 0.0).astype(bf16)
                y_g = y_g + jnp.dot(ms[s][g * heads_per_group + hh], xm, preferred_element_type=f32)
            y_parts.append(y_g)
        ys.append(jnp.concatenate(y_parts, axis=-1))
    ng = ng_ref[...]
    for s in sq:
        y = (ys[s] + dexp_ref[...] * xss[s]) * _silu(zs[s])
        outs = []
        for g in groups:
            yg = y[:, gsl[g]]
            outs.append(yg * lax.rsqrt(jnp.mean(yg * yg, -1, keepdims=True) + 1e-6) * ng[:, gsl[g]])
        y_ref[s] = jnp.concatenate(outs, axis=-1).astype(bf16)

    @pl.when(ci == pl.num_programs(1) - 1)
    def _():
        for s in range(nsq):
            hout_ref[s] = ht_scr[s].T.reshape(SSD_HEADS, HEAD_DIM, SSD_STATE)
        halo_out_ref[...] = halo_scr[...]


def _ssd(p_ssd, halo0, h0, st_layer, lw, layer, nsq, c, valid):
    nseq, lp, _ = p_ssd.shape
    hp = SSD_HEADS * HEAD_DIM
    return pl.pallas_call(
        functools.partial(_ssd_kernel, nsq, c, valid),
        grid=(nseq // nsq, lp // c),
        in_specs=[pl.BlockSpec((nsq, c, P_SSD), lambda b, i: (b, i, 0)),
                  pl.BlockSpec((None, nsq, 8, SSD_CONV_CH), lambda b, i: (st_layer, b, 0, 0)),
                  pl.BlockSpec((None, nsq, SSD_HEADS, HEAD_DIM, SSD_STATE), lambda b, i: (st_layer, b, 0, 0, 0)),
                  _layer_spec(lw["ssd_conv_w"], layer), _layer_spec(lw["ssd_conv_b"], layer),
                  _layer_spec(lw["ssd_dt_bias"], layer), _layer_spec(lw["ssd_a_log"], layer),
                  _layer_spec(lw["ssd_d_x"], layer), _layer_spec(lw["ssd_norm_g"], layer)],
        out_specs=[pl.BlockSpec((nsq, c, hp), lambda b, i: (b, i, 0)),
                   pl.BlockSpec((nsq, SSD_HEADS, HEAD_DIM, SSD_STATE), lambda b, i: (b, 0, 0, 0)),
                   pl.BlockSpec((nsq, 8, SSD_CONV_CH), lambda b, i: (b, 0, 0))],
        out_shape=[jax.ShapeDtypeStruct((nseq, lp, hp), bf16),
                   jax.ShapeDtypeStruct((nseq, SSD_HEADS, HEAD_DIM, SSD_STATE), f32),
                   jax.ShapeDtypeStruct((nseq, 8, SSD_CONV_CH), f32)],
        scratch_shapes=[pltpu.VMEM((nsq, SSD_STATE, hp), f32), pltpu.VMEM((nsq, 8, SSD_CONV_CH), f32)],
        compiler_params=_params(("parallel", "arbitrary")),
        name="ssd",
    )(p_ssd, halo0, h0, lw["ssd_conv_w"], lw["ssd_conv_b"], lw["ssd_dt_bias"], lw["ssd_a_log"],
      lw["ssd_d_x"], lw["ssd_norm_g"])


def _gla_kernel(nsq, c, valid, p_ref, st0_ref, gw_ref, gb_ref, gn_ref, y_ref, st_out_ref, st_scr, q_scr, b_scr,
                o_scr):
    ci = pl.program_id(1)
    hk = GLA_HEADS * GLA_DK
    hv = GLA_HEADS * GLA_DV
    blk = (lax.broadcasted_iota(jnp.int32, (hk, hv), 0) // GLA_DK
           == lax.broadcasted_iota(jnp.int32, (hk, hv), 1) // GLA_DV)

    @pl.when(ci == 0)
    def _():
        for s in range(nsq):
            s2d = st0_ref[s].reshape(hk, GLA_DV)
            st_scr[s] = jnp.where(blk, jnp.concatenate([s2d] * GLA_HEADS, axis=-1), 0.0)

    row = lax.broadcasted_iota(jnp.int32, (c, 1), 0)
    tri = _tri(c)
    tri_b = tri.astype(bf16)
    lane_k = lax.broadcasted_iota(jnp.int32, (1, hk), 1) // GLA_DK
    lane_v = lax.broadcasted_iota(jnp.int32, (1, hv), 1) // GLA_DV
    head_sum = blk.astype(bf16)
    gmat = jnp.where(lax.broadcasted_iota(jnp.int32, (hv, hv), 0) // GLA_DV
                     == lax.broadcasted_iota(jnp.int32, (hv, hv), 1) // GLA_DV, 1.0 / GLA_DV, 0.0).astype(bf16)

    sq = range(nsq)
    ps = [p_ref[s] for s in sq]
    gate = [_dot_split2(ps[s][:, 2 * hk + 2 * hv:2 * hk + 2 * hv + GLA_GATE_RANK], gw_ref[...]) for s in sq]
    ks = [ps[s][:, hk:2 * hk] for s in sq]
    vs = [ps[s][:, 2 * hk:2 * hk + hv] for s in sq]
    log_as = [_log_sigmoid(gate[s] + gb_ref[...]) / GLA_TAU for s in sq]
    if valid < c:
        log_as = [jnp.where(row < valid, la, 0.0) for la in log_as]
        ks = [jnp.where(row < valid, k, 0.0) for k in ks]
    bs = [_dot_f32_rhs(tri_b, la) for la in log_as]
    b_lasts = [b[c - 1:c, :] for b in bs]
    qs = [ps[s][:, 0:hk] * GLA_DK ** -0.5 for s in sq]
    qts = [qs[s] * jnp.exp(bs[s]) for s in sq]
    sts = [st_scr[s] for s in sq]
    o_inters = [_bdot(qts[s], sts[s]) for s in sq]
    upd = [_bdot_tn(ks[s] * jnp.exp(b_lasts[s] - bs[s]), vs[s]) for s in sq]
    for s in sq:
        e_col = jnp.exp(jnp.broadcast_to(b_lasts[s], (8, hk)).T[:, 0:1])
        st_scr[s] = jnp.where(blk, sts[s] * e_col + upd[s], 0.0)
    seqs = [(qs[s], ks[s], vs[s], bs[s], qts[s], o_inters[s], b_lasts[s]) for s in sq]

    def intra_fast():
        kts = [ks[s] * jnp.exp(-bs[s]) for s in sq]
        atts = [[_bdot_nt(jnp.where(lane_k == h, qts[s], 0.0), kts[s]) for h in range(GLA_HEADS)] for s in sq]
        outs = []
        for s in sq:
            o = jnp.zeros((c, hv), f32)
            for h in range(GLA_HEADS):
                o = o + _bdot(jnp.where(tri, atts[s][h], 0.0), jnp.where(lane_v == h, vs[s], 0.0))
            outs.append(o)
        return outs

    def intra_exact():
        outs = []
        for (q, k, v, b, _, _, _) in seqs:
            q_scr[...] = q
            b_scr[...] = b

            def body(i, carry, k=k, v=v, b=b):
                qi = q_scr[pl.ds(i, 1), :]
                bi = b_scr[pl.ds(i, 1), :]
                t = jnp.where(row <= i, qi * k * jnp.exp(jnp.minimum(bi - b, 0.0)), 0.0)
                wgt = jnp.dot(t.astype(bf16), head_sum, preferred_element_type=f32)
                o_scr[pl.ds(i, 1), :] = jnp.sum(wgt * v, axis=0, keepdims=True)
                return carry

            lax.fori_loop(0, c, body, 0)
            outs.append(o_scr[...])
        return outs

    min_decay = b_lasts[0]
    for bl in b_lasts[1:]:
        min_decay = jnp.minimum(min_decay, bl)
    intra = lax.cond(jnp.min(min_decay) > GLA_SAFE_LOG_DECAY, intra_fast, intra_exact)

    os_ = [o_inters[s] + intra[s] for s in sq]
    mss = []
    for s in sq:
        oo = os_[s] * os_[s]
        oo_hi = oo.astype(bf16)
        oo_lo = (oo - oo_hi.astype(f32)).astype(bf16)
        mss.append(jnp.dot(oo_hi, gmat, preferred_element_type=f32)
                   + jnp.dot(oo_lo, gmat, preferred_element_type=f32))
    for s in sq:
        r = ps[s][:, 2 * hk + hv:2 * hk + 2 * hv]
        y_ref[s] = (os_[s] * lax.rsqrt(mss[s] + 1e-6) * gn_ref[...] * _silu(r)).astype(bf16)

    @pl.when(ci == pl.num_programs(1) - 1)
    def _():
        for s in range(nsq):
            st = st_scr[s]
            out = st[:, 0:GLA_DV]
            for h in range(1, GLA_HEADS):
                out = out + st[:, h * GLA_DV:(h + 1) * GLA_DV]
            st_out_ref[s] = out.reshape(GLA_HEADS, GLA_DK, GLA_DV)


def _gla(p_gla, st0, st_layer, lw, layer, nsq, c, valid):
    nseq, lp, _ = p_gla.shape
    hk = GLA_HEADS * GLA_DK
    hv = GLA_HEADS * GLA_DV
    return pl.pallas_call(
        functools.partial(_gla_kernel, nsq, c, valid),
        grid=(nseq // nsq, lp // c),
        in_specs=[pl.BlockSpec((nsq, c, P_GLA), lambda b, i: (b, i, 0)),
                  pl.BlockSpec((None, nsq, GLA_HEADS, GLA_DK, GLA_DV), lambda b, i: (st_layer, b, 0, 0, 0)),
                  _layer_spec(lw["gla_gate_w"], layer), _layer_spec(lw["gla_gate_b"], layer),
                  _layer_spec(lw["gla_norm_g_x"], layer)],
        out_specs=[pl.BlockSpec((nsq, c, hv), lambda b, i: (b, i, 0)),
                   pl.BlockSpec((nsq, GLA_HEADS, GLA_DK, GLA_DV), lambda b, i: (b, 0, 0, 0))],
        out_shape=[jax.ShapeDtypeStruct((nseq, lp, hv), bf16),
                   jax.ShapeDtypeStruct((nseq, GLA_HEADS, GLA_DK, GLA_DV), f32)],
        scratch_shapes=[pltpu.VMEM((nsq, hk, hv), f32), pltpu.VMEM((c, hk), f32), pltpu.VMEM((c, hk), f32),
                        pltpu.VMEM((c, hv), f32)],
        compiler_params=_params(("parallel", "arbitrary")),
        name="gla",
    )(p_gla, st0, lw["gla_gate_w"], lw["gla_gate_b"], lw["gla_norm_g_x"])


def _mla_prep_kernel(nsb, ts, with_kt, p_ref, cos_ref, sin_ref, qg_ref, kvg_ref, wuq_ref, wcat_ref, sel_ref,
                     qcat_ref, kcat_ref, ckv_ref, krope_ref, *kt_ref):
    rows = nsb * ts
    p = p_ref[...].reshape(rows, P_MLA)
    cq = p[:, 0:256]
    ckv = p[:, 256:384]
    kra = p[:, 384:512]
    krb = p[:, 512:640]
    cos = cos_ref[...]
    sin = sin_ref[...]
    cqn = cq * lax.rsqrt(jnp.sum(cq * cq, -1, keepdims=True) * (1.0 / MLA_Q_LORA) + 1e-6) * qg_ref[...]
    qf = _bdot(cqn, wuq_ref[...])
    ra = qf[:, 256:384]
    rb = qf[:, 384:512]
    feat = jnp.concatenate([qf[:, 0:256], ra * cos - rb * sin, ra * sin + rb * cos], axis=-1).astype(bf16)
    for h in range(MLA_HEADS):
        qh = jnp.dot(feat, wcat_ref[h], preferred_element_type=f32).astype(bf16)
        qcat_ref[:, h] = qh.reshape(nsb, ts, QK_W)
    c_kv = ckv * lax.rsqrt(jnp.mean(ckv * ckv, -1, keepdims=True) + 1e-6) * kvg_ref[...]
    kr2 = jnp.concatenate([kra * cos - krb * sin, kra * sin + krb * cos], axis=-1)
    kr128 = _dot_f32_lhs(kr2, sel_ref[...])
    ckv_ref[...] = c_kv.reshape(nsb, ts, MLA_KV_LORA)
    krope_ref[...] = kr128[:, 0:MLA_ROPE].reshape(nsb, ts, MLA_ROPE)
    one_lane = (lax.broadcasted_iota(jnp.int32, (1, 128), 1) == 127).astype(f32)
    kcat = jnp.concatenate([c_kv, kr128 + one_lane], axis=-1)
    kcat_ref[...] = kcat.astype(bf16).reshape(nsb, ts, QK_W)
    if with_kt:
        kt_ref[0][0] = kcat.T.astype(bf16)


def _mla_prep(p_mla, cos, sin, lw, layer, nsb, ts, with_kt):
    nseq, lp, _ = p_mla.shape
    rows = nsb * ts
    ntile = lp // ts
    tbl_tiles = cos.shape[0] // rows
    out_specs = [pl.BlockSpec((nsb, MLA_HEADS, ts, QK_W), lambda b, i: (b, 0, i, 0)),
                 pl.BlockSpec((nsb, ts, QK_W), lambda b, i: (b, i, 0)),
                 pl.BlockSpec((nsb, ts, MLA_KV_LORA), lambda b, i: (b, i, 0)),
                 pl.BlockSpec((nsb, ts, MLA_ROPE), lambda b, i: (b, i, 0))]
    out_shape = [jax.ShapeDtypeStruct((nseq, MLA_HEADS, lp, QK_W), bf16),
                 jax.ShapeDtypeStruct((nseq, lp, QK_W), bf16),
                 jax.ShapeDtypeStruct((nseq, lp, MLA_KV_LORA), f32),
                 jax.ShapeDtypeStruct((nseq, lp, MLA_ROPE), f32)]
    if with_kt:
        assert nsb == 1
        out_specs.append(pl.BlockSpec((1, QK_W, ts), lambda b, i: (b, 0, i)))
        out_shape.append(jax.ShapeDtypeStruct((nseq, QK_W, lp), bf16))
    return pl.pallas_call(
        functools.partial(_mla_prep_kernel, nsb, ts, with_kt),
        grid=(nseq // nsb, ntile),
        in_specs=[pl.BlockSpec((nsb, ts, P_MLA), lambda b, i: (b, i, 0)),
                  pl.BlockSpec((rows, 128), lambda b, i: (i % tbl_tiles, 0)),
                  pl.BlockSpec((rows, 128), lambda b, i: (i % tbl_tiles, 0)),
                  _layer_spec(lw["mla_q_norm_g_x"], layer), _layer_spec(lw["mla_kv_norm_g"], layer),
                  _layer_spec(lw["mla_w_uq_x"], layer), _layer_spec(lw["mla_wcat"], layer),
                  _const_spec((256, 128))],
        out_specs=out_specs,
        out_shape=out_shape,
        compiler_params=_params(("parallel", "parallel")),
        name="mla_prep",
    )(p_mla, cos, sin, lw["mla_q_norm_g_x"], lw["mla_kv_norm_g"], lw["mla_w_uq_x"], lw["mla_wcat"],
      lw["mla_sel"])


def _mla_flash_kernel(tq, tk, qi_ref, ki_ref, q_ref, k_ref, kt_ref, wuvt_ref, y_ref, m_scr, acc_scr):
    qi = qi_ref[pl.program_id(1)]
    ki = ki_ref[pl.program_id(1)]

    @pl.when(ki == 0)
    def _():
        m_scr[...] = jnp.full(m_scr.shape, -jnp.inf, f32)
        acc_scr[...] = jnp.zeros(acc_scr.shape, f32)

    def step(masked):
        k = k_ref[0]
        kt = kt_ref[0]
        heads = range(MLA_HEADS)
        sts = [lax.dot_general(k, q_ref[0, h], (((1,), (1,)), ((), ())), preferred_element_type=f32)
               for h in heads]
        if masked:
            keep = (lax.broadcasted_iota(jnp.int32, (tk, tq), 0) <= lax.broadcasted_iota(jnp.int32, (tk, tq), 1))
            sts = [jnp.where(keep, st, -jnp.inf) for st in sts]
        alphas, pts = [], []
        for h in heads:
            m_prev = m_scr[h]
            m_new = jnp.maximum(m_prev, jnp.max(sts[h], 0, keepdims=True))
            alphas.append(jnp.exp2((m_prev - m_new) * EXP2_SCALE))
            pts.append(jnp.exp2((sts[h] - m_new) * EXP2_SCALE).astype(bf16))
            m_scr[h] = m_new
        for h in heads:
            acc_scr[h] = alphas[h] * acc_scr[h] + jnp.dot(kt, pts[h], preferred_element_type=f32)

    @pl.when(ki < qi)
    def _():
        step(False)

    @pl.when(ki == qi)
    def _():
        step(True)
        yt = jnp.zeros((MLA_HEADS * MLA_DV, tq), f32)
        for h in range(MLA_HEADS):
            acc = acc_scr[h]
            o_t = acc[0:MLA_KV_LORA, :] / acc[QK_W - 1:QK_W, :]
            yt = yt + jnp.dot(wuvt_ref[h], o_t.astype(bf16), preferred_element_type=f32)
        y_ref[0] = yt.T.astype(bf16)


def _mla_flash(qcat, kcat, kcat_t, lw, layer, tq):
    nseq, _, lp, _ = qcat.shape
    tk = tq
    nq = lp // tq
    pairs = [(i, j) for i in range(nq) for j in range(i + 1)]
    qi_tab = jnp.asarray([p[0] for p in pairs], jnp.int32)
    ki_tab = jnp.asarray([p[1] for p in pairs], jnp.int32)
    wuvt = lw["mla_wuvt_x"]
    grid_spec = pltpu.PrefetchScalarGridSpec(
        num_scalar_prefetch=2,
        grid=(nseq, len(pairs)),
        in_specs=[pl.BlockSpec((1, MLA_HEADS, tq, QK_W), lambda b, t, qi, ki: (b, 0, qi[t], 0)),
                  pl.BlockSpec((1, tk, QK_W), lambda b, t, qi, ki: (b, ki[t], 0)),
                  pl.BlockSpec((1, QK_W, tk), lambda b, t, qi, ki: (b, 0, ki[t])),
                  pl.BlockSpec((None,) + wuvt.shape[1:], lambda b, t, qi, ki: (layer, 0, 0, 0))],
        out_specs=pl.BlockSpec((1, tq, MLA_HEADS * MLA_DV), lambda b, t, qi, ki: (b, qi[t], 0)),
        scratch_shapes=[pltpu.VMEM((MLA_HEADS, 1, tq), f32), pltpu.VMEM((MLA_HEADS, QK_W, tq), f32)],
    )
    return pl.pallas_call(
        functools.partial(_mla_flash_kernel, tq, tk),
        grid_spec=grid_spec,
        out_shape=jax.ShapeDtypeStruct((nseq, lp, MLA_HEADS * MLA_DV), bf16),
        compiler_params=_params(("parallel", "arbitrary")),
        name="mla_flash",
    )(qi_tab, ki_tab, qcat, kcat, kcat_t, wuvt)


def _mla_paged_kernel(valid, n_pages, layer, pt_ref, q_ref, knew_ref, wuv_ref, kv_hbm, krt_hbm, y_ref,
                      kv_buf, krt_buf, kvb_scr, s_scr, sem):
    b = pl.program_id(0)
    nb = pl.num_programs(0)
    m_rows = MLA_HEADS * SEG_PAD
    t_past = n_pages * PAGE_SIZE
    chunk = math.gcd(t_past, PAGED_CHUNK)

    def page_copies(seq, i, slot):
        pg = pt_ref[seq, i]
        rows = pl.ds(i * PAGE_SIZE, PAGE_SIZE)
        return (pltpu.make_async_copy(kv_hbm.at[layer, pg], kv_buf.at[slot, rows, :], sem.at[0, slot]),
                pltpu.make_async_copy(krt_hbm.at[layer, pg], krt_buf.at[slot, :, rows], sem.at[1, slot]))

    def start_seq(seq, slot):
        for i in range(n_pages):
            kv_cp, kr_cp = page_copies(seq, i, slot)
            kv_cp.start(priority=0)
            kr_cp.start(priority=1)

    def wait_seq(seq, slot):
        for i in range(n_pages):
            for cp in page_copies(seq, i, slot):
                cp.wait()

    @pl.when(b == 0)
    def _():
        start_seq(0, 0)
        start_seq(jnp.minimum(1, nb - 1), 1)

    slot = b % PAGED_SLOTS
    wait_seq(b, slot)
    start_seq(jnp.minimum(b + 2, nb - 1), (b + 2) % PAGED_SLOTS)

    q = q_ref[0].reshape(m_rows, QK_W)
    q_lat = q[:, 0:MLA_KV_LORA]
    q_rope = q[:, MLA_KV_LORA:MLA_KV_LORA + MLA_ROPE]
    for ci in range(t_past // chunk):
        cols = pl.ds(ci * chunk, chunk)
        kvc = kv_buf[slot, cols, :].astype(bf16)
        kvb_scr[cols, :] = kvc
        s_scr[:, cols] = (lax.dot_general(q_lat, kvc, (((1,), (1,)), ((), ())), preferred_element_type=f32)
                          + jnp.dot(q_rope, krt_buf[slot, :, cols].astype(bf16), preferred_element_type=f32))

    knew = knew_ref[0]
    sn = lax.dot_general(q, knew, (((1,), (1,)), ((), ())), preferred_element_type=f32)
    rpos = lax.broadcasted_iota(jnp.int32, (MLA_HEADS, SEG_PAD, SEG_PAD), 1).reshape(m_rows, SEG_PAD)
    cpos = lax.broadcasted_iota(jnp.int32, (m_rows, SEG_PAD), 1)
    sn = jnp.where((cpos <= rpos) & (cpos < valid), sn, -jnp.inf)
    m = jnp.maximum(jnp.max(s_scr[...], -1, keepdims=True), jnp.max(sn, -1, keepdims=True))
    pn = jnp.exp((sn - m) * MLA_SCALE)
    l = jnp.sum(pn, -1, keepdims=True)
    acc = jnp.dot(pn.astype(bf16), knew[:, 0:MLA_KV_LORA], preferred_element_type=f32)
    for ci in range(t_past // chunk):
        cols = pl.ds(ci * chunk, chunk)
        pr = jnp.exp((s_scr[:, cols] - m) * MLA_SCALE)
        l = l + jnp.sum(pr, -1, keepdims=True)
        acc = acc + jnp.dot(pr.astype(bf16), kvb_scr[cols, :], preferred_element_type=f32)
    o_lat = acc / l
    y = jnp.zeros((SEG_PAD, MLA_HEADS * MLA_DV), f32)
    for h in range(MLA_HEADS):
        y = y + _bdot(o_lat[h * SEG_PAD:(h + 1) * SEG_PAD, :], wuv_ref[h])
    y_ref[0] = y.astype(bf16)

    @pl.when(b == nb - 1)
    def _():
        wait_seq(b, (b + 1) % PAGED_SLOTS)
        wait_seq(b, (b + 2) % PAGED_SLOTS)


def _mla_paged(qcat, kcat, cache_kv, cache_krt, page_table, lw, layer, valid):
    nseq = qcat.shape[0]
    n_pages = page_table.shape[1]
    t_past = n_pages * PAGE_SIZE
    wuv = lw["mla_wuv_x"]
    grid_spec = pltpu.PrefetchScalarGridSpec(
        num_scalar_prefetch=1,
        grid=(nseq,),
        in_specs=[pl.BlockSpec((1, MLA_HEADS, SEG_PAD, QK_W), lambda b, pt: (b, 0, 0, 0)),
                  pl.BlockSpec((1, SEG_PAD, QK_W), lambda b, pt: (b, 0, 0)),
                  pl.BlockSpec((None,) + wuv.shape[1:], lambda b, pt: (layer, 0, 0, 0)),
                  pl.BlockSpec(memory_space=pl.ANY), pl.BlockSpec(memory_space=pl.ANY)],
        out_specs=pl.BlockSpec((1, SEG_PAD, MLA_HEADS * MLA_DV), lambda b, pt: (b, 0, 0)),
        scratch_shapes=[pltpu.VMEM((PAGED_SLOTS, t_past, MLA_KV_LORA), f32),
                        pltpu.VMEM((PAGED_SLOTS, MLA_ROPE, t_past), f32),
                        pltpu.VMEM((t_past, MLA_KV_LORA), bf16),
                        pltpu.VMEM((MLA_HEADS * SEG_PAD, t_past), f32),
                        pltpu.SemaphoreType.DMA((2, PAGED_SLOTS))],
    )
    return pl.pallas_call(
        functools.partial(_mla_paged_kernel, valid, n_pages, layer),
        grid_spec=grid_spec,
        out_shape=jax.ShapeDtypeStruct((nseq, SEG_PAD, MLA_HEADS * MLA_DV), bf16),
        compiler_params=_params(("arbitrary",)),
        name="mla_paged",
    )(page_table, qcat, kcat, wuv, cache_kv, cache_krt)


def _ffn_kernel(nsb, ts, valid, x_ref, ys_ref, yg_ref, ym_ref, pe_ref, halo0_ref,
                wout_ref, g1_ref, b1_ref, wup_ref, fcw_ref, fcb_ref, wdn_ref, wpg_ref, wpp_ref, g2_ref, b2_ref,
                o_ref, halo_out_ref, halo_scr, hmid_scr):
    ti = pl.program_id(1)
    rows = nsb * ts

    @pl.when(ti == 0)
    def _():
        halo_scr[...] = halo0_ref[...]

    x = x_ref[...].reshape(rows, D_MODEL)
    ycat = jnp.concatenate([ys_ref[...].reshape(rows, SSD_WIDTH), yg_ref[...].reshape(rows, GLA_WIDTH),
                            ym_ref[...].reshape(rows, MLA_HEADS * MLA_DV)], axis=-1)
    mix = jnp.dot(ycat, wout_ref[...], preferred_element_type=f32)
    x1 = _layernorm(DN_ALPHA * x + mix, g1_ref[...], b1_ref[...])
    x1b = x1.astype(bf16)

    pos = lax.broadcasted_iota(jnp.int32, (nsb, ts, 1), 1)
    for ch in range(D_FF // FF_CHUNK):
        lo = ch * FF_CHUNK
        a = jnp.dot(x1b, wup_ref[:, lo:lo + FF_CHUNK], preferred_element_type=f32)
        bgate = jnp.dot(x1b, wup_ref[:, D_FF + lo:D_FF + lo + FF_CHUNK], preferred_element_type=f32)
        a3 = a.reshape(nsb, ts, FF_CHUNK)
        halo = halo_scr[:, :, lo:lo + FF_CHUNK]
        hm1, hm2 = halo[:, 7:8, :], halo[:, 6:7, :]
        s1 = jnp.where(pos >= 1, pltpu.roll(a3, 1, axis=1), hm1)
        s2 = jnp.where(pos >= 2, pltpu.roll(a3, 2, axis=1), jnp.where(pos == 1, hm1, hm2))
        fcw = fcw_ref[:, lo:lo + FF_CHUNK]
        ac = fcw[2:3, :] * a3 + fcw[1:2, :] * s1 + fcw[0:1, :] * s2 + fcb_ref[:, lo:lo + FF_CHUNK]
        last = a3[:, ts - 8:ts, :]
        if valid < ts:
            last = pltpu.roll(last, ts - valid, axis=1)
        halo_scr[:, :, lo:lo + FF_CHUNK] = last
        ge = 0.5 * ac * (1.0 + lax.erf(ac * (1.0 / math.sqrt(2.0))))
        hmid_scr[:, lo:lo + FF_CHUNK] = (ge.reshape(rows, FF_CHUNK) * bgate).astype(bf16)
    f = jnp.dot(hmid_scr[...], wdn_ref[...], preferred_element_type=f32)

    gate = jax.nn.sigmoid(jnp.dot(x1b, wpg_ref[...], preferred_element_type=f32))
    pe = _bdot(pe_ref[...].reshape(rows, PLE_DIM), wpp_ref[...])
    x2 = _layernorm(DN_ALPHA * x1 + f + gate * pe, g2_ref[...], b2_ref[...])
    o_ref[...] = x2.reshape(nsb, ts, D_MODEL)

    @pl.when(ti == pl.num_programs(1) - 1)
    def _():
        halo_out_ref[...] = halo_scr[...]


def _ffn(x3, ys, yg, ym, pe4, halo0, st_layer, lw, layer, nsb, ts, valid):
    nseq, lp, _ = x3.shape

    def tok(width):
        return pl.BlockSpec((nsb, ts, width), lambda b, i: (b, i, 0))

    names = ["w_out", "ln1_g", "ln1_b", "ffn_w_up", "ffn_conv_w", "ffn_conv_b", "ffn_w_down", "ple_w_gate",
             "ple_w_proj", "ln2_g", "ln2_b"]
    return pl.pallas_call(
        functools.partial(_ffn_kernel, nsb, ts, valid),
        grid=(nseq // nsb, lp // ts),
        in_specs=[tok(D_MODEL), tok(SSD_WIDTH), tok(GLA_WIDTH), tok(MLA_HEADS * MLA_DV),
                  pl.BlockSpec((None, nsb, ts, PLE_DIM), lambda b, i: (layer, b, i, 0)),
                  pl.BlockSpec((None, nsb, 8, D_FF), lambda b, i: (st_layer, b, 0, 0))]
        + [_layer_spec(lw[n], layer) for n in names],
        out_specs=[tok(D_MODEL), pl.BlockSpec((nsb, 8, D_FF), lambda b, i: (b, 0, 0))],
        out_shape=[jax.ShapeDtypeStruct((nseq, lp, D_MODEL), f32),
                   jax.ShapeDtypeStruct((nseq, 8, D_FF), f32)],
        scratch_shapes=[pltpu.VMEM((nsb, 8, D_FF), f32), pltpu.VMEM((nsb * ts, D_FF), bf16)],
        compiler_params=_params(("parallel", "arbitrary")),
        name="out_ffn",
    )(x3, ys, yg, ym, pe4, halo0, *[lw[n] for n in names])


def _prep_weights(w):
    depth = w["w_in"].shape[0]

    def row(v):
        return v.reshape(depth, 1, -1).astype(f32)

    splits = np.cumsum([0, SSD_WIDTH, SSD_CONV_CH, SSD_HEADS, GLA_HEADS * GLA_DK, GLA_HEADS * GLA_DK, GLA_WIDTH,
                        GLA_WIDTH, GLA_GATE_RANK, MLA_Q_LORA, MLA_KV_LORA, MLA_ROPE]).tolist()
    w_in = w["w_in"]
    (z, xbc, dt, q, k, v, r, glr, cq, ckv, kr) = [w_in[:, :, splits[j]:splits[j + 1]] for j in range(11)]

    def zpad(n):
        return jnp.zeros((depth, D_MODEL, n), f32)

    half = MLA_ROPE // 2
    w_all = jnp.concatenate(
        [z, xbc, dt, zpad(P_SSD - SSD_WIDTH - SSD_CONV_CH - SSD_HEADS),
         q, k, v, r, glr, zpad(P_GLA - 2 * GLA_HEADS * GLA_DK - 2 * GLA_WIDTH - GLA_GATE_RANK),
         cq, zpad(256 - MLA_Q_LORA), ckv, kr[:, :, :half], zpad(128 - half), kr[:, :, half:], zpad(128 - half)],
        axis=2).astype(bf16)

    wuq = w["mla_w_uq"].reshape(depth, MLA_Q_LORA, MLA_HEADS, MLA_NOPE + MLA_ROPE)
    nope = wuq[..., :MLA_NOPE].reshape(depth, MLA_Q_LORA, MLA_HEADS * MLA_NOPE)
    ra = wuq[..., MLA_NOPE:MLA_NOPE + half].reshape(depth, MLA_Q_LORA, MLA_HEADS * half)
    rb = wuq[..., MLA_NOPE + half:].reshape(depth, MLA_Q_LORA, MLA_HEADS * half)
    zq = jnp.zeros((depth, MLA_Q_LORA, 128 - MLA_HEADS * half), f32)
    wuq_x = jnp.concatenate([nope, ra, zq, rb, zq], axis=2)
    wuq_x = jnp.concatenate([wuq_x, jnp.zeros((depth, 256 - MLA_Q_LORA, 512), f32)], axis=1).astype(bf16)

    wuk = w["mla_w_uk"]
    wcat = jnp.zeros((depth, MLA_HEADS, 512, QK_W), f32)
    eye = jnp.eye(half, dtype=f32)
    for h in range(MLA_HEADS):
        wcat = wcat.at[:, h, h * MLA_NOPE:(h + 1) * MLA_NOPE, 0:MLA_KV_LORA].set(jnp.swapaxes(wuk[:, :, h, :], 1, 2))
        wcat = wcat.at[:, h, 256 + h * half:256 + (h + 1) * half, MLA_KV_LORA:MLA_KV_LORA + half].set(eye)
        wcat = wcat.at[:, h, 384 + h * half:384 + (h + 1) * half,
                       MLA_KV_LORA + half:MLA_KV_LORA + 2 * half].set(eye)
    sel = jnp.zeros((256, 128), f32)
    sel = sel.at[0:half, 0:half].set(eye).at[128:128 + half, half:2 * half].set(eye)
    wuv = w["mla_w_uv"]
    wuv_x = jnp.zeros((depth, MLA_HEADS, MLA_KV_LORA, MLA_HEADS * MLA_DV), f32)
    for h in range(MLA_HEADS):
        wuv_x = wuv_x.at[:, h, :, h * MLA_DV:(h + 1) * MLA_DV].set(wuv[:, :, h, :])

    return {
        "w_in": w_all,
        "ssd_conv_w": w["ssd_conv_w"], "ssd_conv_b": row(w["ssd_conv_b"]),
        "ssd_dt_bias": row(w["ssd_dt_bias"]), "ssd_a_log": row(w["ssd_a_log"]),
        "ssd_d_x": row(jnp.repeat(w["ssd_d"], HEAD_DIM, axis=1)), "ssd_norm_g": row(w["ssd_norm_g"]),
        "gla_gate_w": w["gla_gate_w"], "gla_gate_b": row(w["gla_gate_b"]),
        "gla_norm_g_x": row(jnp.tile(w["gla_norm_g"], (1, GLA_HEADS))),
        "mla_q_norm_g_x": row(jnp.concatenate([w["mla_q_norm_g"], jnp.zeros((depth, 256 - MLA_Q_LORA), f32)], axis=1)),
        "mla_kv_norm_g": row(w["mla_kv_norm_g"]),
        "mla_w_uq_x": wuq_x, "mla_wcat": wcat.astype(bf16), "mla_sel": sel, "mla_wuv_x": wuv_x.astype(bf16),
        "mla_wuvt_x": jnp.swapaxes(wuv_x, 2, 3).astype(bf16),
        "w_out": w["w_out"].astype(bf16), "ln1_g": row(w["ln1_g"]), "ln1_b": row(w["ln1_b"]),
        "ffn_w_up": w["ffn_w_up"].astype(bf16), "ffn_conv_w": w["ffn_conv_w"],
        "ffn_conv_b": row(w["ffn_conv_b"]), "ffn_w_down": w["ffn_w_down"].astype(bf16),
        "ple_w_gate": w["ple_w_gate"].astype(bf16), "ple_w_proj": w["ple_w_proj"].astype(bf16),
        "ln2_g": row(w["ln2_g"]), "ln2_b": row(w["ln2_b"]),
    }


def _rope_tables(pos, rows):
    half = MLA_ROPE // 2
    inv = ROPE_THETA ** (-jnp.arange(half, dtype=f32) / half)
    ang = pos.astype(f32)[:, None] * inv
    pad = jnp.zeros((pos.shape[0], 128 - MLA_HEADS * half), f32)
    cos = jnp.concatenate([jnp.tile(jnp.cos(ang), (1, MLA_HEADS)), pad], axis=1)
    sin = jnp.concatenate([jnp.tile(jnp.sin(ang), (1, MLA_HEADS)), pad], axis=1)
    reps = max(1, rows // pos.shape[0])
    return jnp.tile(cos, (reps, 1)), jnp.tile(sin, (reps, 1))


def _halo(buf):
    lead = buf.shape[:-2]
    wdt, ch = buf.shape[-2:]
    return jnp.concatenate([jnp.zeros(lead + (8 - wdt, ch), f32), buf], axis=-2)


def _layer(x3, pe4, layer, states, st_layer, lw, cfg, cache=None):
    nseq, lp, _ = x3.shape
    c, nsq, valid, ts, ts_ffn, nsb = cfg["c"], cfg["nsq"], cfg["valid"], cfg["ts"], cfg["ts_ffn"], cfg["nsb"]
    ssm0, conv_halo0, gla0, ffn_halo0 = states
    rows = nseq * lp
    p_ssd, p_gla, p_mla = _in_proj(x3.reshape(rows, D_MODEL), lw, layer, min(rows, 1024))
    y_ssd, ssm_h, conv_halo = _ssd(p_ssd.reshape(nseq, lp, P_SSD), conv_halo0, ssm0, st_layer, lw, layer, nsq, c,
                                   valid)
    y_gla, gla_s = _gla(p_gla.reshape(nseq, lp, P_GLA), gla0, st_layer, lw, layer, nsq, c, valid)
    prep = _mla_prep(p_mla.reshape(nseq, lp, P_MLA), cfg["cos"], cfg["sin"], lw, layer, nsb, ts, cache is None)
    qcat, kcat, c_kv, k_rope = prep[:4]
    if cache is None:
        y_mla = _mla_flash(qcat, kcat, prep[4], lw, layer, ts)
    else:
        cache_kv, cache_krt, page_table = cache
        y_mla = _mla_paged(qcat, kcat, cache_kv, cache_krt, page_table, lw, layer, valid)
    x_out, ffn_halo = _ffn(x3, y_ssd, y_gla, y_mla, pe4, ffn_halo0, st_layer, lw, layer, nsb, ts_ffn, valid)
    out_states = (c_kv[:, :valid], k_rope[:, :valid], ssm_h, conv_halo[:, 8 - (SSD_CONV - 1):], gla_s,
                  ffn_halo[:, 8 - (FFN_CONV - 1):])
    return x_out, out_states


def _pad_rows(a, lp):
    nseq, l = a.shape[:2]
    if l == lp:
        return a
    return jnp.concatenate([a, jnp.zeros((nseq, lp - l) + a.shape[2:], a.dtype)], axis=1)


def kernel(x_prompt, x_sample, cache_kv_latent, cache_k_rope, state_ssm, state_ssm_conv, state_gla, state_ffn_conv, page_table, p_prompt, p_sample, ln_in_g, ln_in_b, w_in, ssd_conv_w, ssd_conv_b, ssd_dt_bias, ssd_a_log, ssd_d, ssd_norm_g, gla_gate_w, gla_gate_b, gla_norm_g, mla_q_norm_g, mla_w_uq, mla_kv_norm_g, mla_w_uk, mla_w_uv, w_out, ln1_g, ln1_b, ffn_w_up, ffn_conv_w, ffn_conv_b, ffn_w_down, ple_w_proj, ple_w_gate, ln2_g, ln2_b):
    weights = dict(w_in=w_in, ssd_conv_w=ssd_conv_w, ssd_conv_b=ssd_conv_b, ssd_dt_bias=ssd_dt_bias,
                   ssd_a_log=ssd_a_log, ssd_d=ssd_d, ssd_norm_g=ssd_norm_g, gla_gate_w=gla_gate_w,
                   gla_gate_b=gla_gate_b, gla_norm_g=gla_norm_g, mla_q_norm_g=mla_q_norm_g, mla_w_uq=mla_w_uq,
                   mla_kv_norm_g=mla_kv_norm_g, mla_w_uk=mla_w_uk, mla_w_uv=mla_w_uv, w_out=w_out, ln1_g=ln1_g,
                   ln1_b=ln1_b, ffn_w_up=ffn_w_up, ffn_conv_w=ffn_conv_w, ffn_conv_b=ffn_conv_b,
                   ffn_w_down=ffn_w_down, ple_w_proj=ple_w_proj, ple_w_gate=ple_w_gate, ln2_g=ln2_g, ln2_b=ln2_b)
    depth = w_in.shape[0]
    bp, lp_, _ = x_prompt.shape
    bs, ls, _ = x_sample.shape
    t_past = page_table.shape[1] * PAGE_SIZE
    lw = _prep_weights(weights)

    g_in = ln_in_g.reshape(1, -1)
    b_in = ln_in_b.reshape(1, -1)
    hp = _ln_in(x_prompt.reshape(bp * lp_, D_MODEL), g_in, b_in, min(bp * lp_, 1024)).reshape(bp, lp_, D_MODEL)
    xs_pad = _pad_rows(x_sample, SEG_PAD)
    hs = _ln_in(xs_pad.reshape(bs * SEG_PAD, D_MODEL), g_in, b_in, bs * SEG_PAD).reshape(bs, SEG_PAD, D_MODEL)

    ts_p = min(lp_, 512)
    cos_p, sin_p = _rope_tables(jnp.arange(lp_), ts_p)
    cfg_p = dict(c=min(lp_, 128), nsq=math.gcd(bp, 8), valid=lp_, ts=ts_p, ts_ffn=ts_p, nsb=1,
                 cos=cos_p, sin=sin_p)
    cos_s, sin_s = _rope_tables(t_past + jnp.arange(SEG_PAD), bs * SEG_PAD)
    cfg_s = dict(c=SEG_PAD, nsq=math.gcd(bs, 8), valid=ls, ts=SEG_PAD, ts_ffn=SEG_PAD, nsb=bs, cos=cos_s, sin=sin_s)

    states_p = (jnp.zeros((1, bp, SSD_HEADS, HEAD_DIM, SSD_STATE), f32), jnp.zeros((1, bp, 8, SSD_CONV_CH), f32),
                jnp.zeros((1, bp, GLA_HEADS, GLA_DK, GLA_DV), f32), jnp.zeros((1, bp, 8, D_FF), f32))
    states_s = (state_ssm, _halo(state_ssm_conv), state_gla, _halo(state_ffn_conv))
    pe_s = jnp.concatenate([p_sample, jnp.zeros((depth, bs, SEG_PAD - ls, PLE_DIM), f32)], axis=2)
    cache_krt = jnp.swapaxes(cache_k_rope, 2, 3)

    prompt_states, sample_states = [], []
    for i in range(depth):
        hp, sp = _layer(hp, p_prompt, i, states_p, 0, lw, cfg_p)
        hs, ss = _layer(hs, pe_s, i, states_s, i, lw, cfg_s, cache=(cache_kv_latent, cache_krt, page_table))
        prompt_states.append(sp)
        sample_states.append(ss)
    kv_p, kr_p, ssm_p, ssm_conv_p, gla_p, ffn_conv_p = (jnp.stack(f) for f in zip(*prompt_states))
    kv_s, kr_s, ssm_s, ssm_conv_s, gla_s, ffn_conv_s = (jnp.stack(f) for f in zip(*sample_states))
    return (hp, hs[:, :ls], kv_p, kv_s, kr_p, kr_s, ssm_p, ssm_s, ssm_conv_p, ssm_conv_s, gla_p, gla_s,
            ffn_conv_p, ffn_conv_s)
```
